```python
import math
import jax
import jax.numpy as jnp
from jax import lax
import numpy as np

D_MODEL = 2048
BATCH = 1
SEQ = 8192
DEPTH = 4

N_MIXERS = 3
N_META = 16
DN_ALPHA = (2.0 * DEPTH) ** 0.25
DN_BETA = (8.0 * DEPTH) ** -0.25
LN_EPS = 1e-5
RMS_EPS = 1e-6
SHORT_CONV = 5

GDN_QK_HEADS = 16
GDN_V_HEADS = 32
GDN_HEAD_DIM = 128
GDN_QK_DIM = GDN_QK_HEADS * GDN_HEAD_DIM
GDN_V_DIM = GDN_V_HEADS * GDN_HEAD_DIM
GDN_CONV_CH = 2 * GDN_QK_DIM + GDN_V_DIM
GDN_IN = GDN_CONV_CH + GDN_V_DIM + 4 * GDN_V_HEADS
GDN_CHUNK = 64

SSD_D_INNER = 2 * D_MODEL
SSD_HEAD_DIM = 64
SSD_HEADS = SSD_D_INNER // SSD_HEAD_DIM
SSD_GROUPS = 8
SSD_STATE = 128
SSD_CONV_CH = SSD_D_INNER + 2 * SSD_GROUPS * SSD_STATE
SSD_IN = SSD_D_INNER + SSD_CONV_CH + 2 * SSD_HEADS
SSD_CHUNK = 128

HY_WIDTH = D_MODEL
HY_ORDER = 2
HY_DIRS = 2
HY_SHORT = 3
HY_EMB = 33
HY_BANDS = (HY_EMB - 1) // 2
HY_FILTER_HIDDEN = 64
HY_TARGET = 1e-2
HY_DECAY_PCT_SHORT = 0.3
HY_DECAY_PCT_LONG = 1.5

MOE_GROUPS = 4
MOE_PER_GROUP = 8
MOE_EXPERTS = MOE_GROUPS * MOE_PER_GROUP
MOE_TOPK = 2
MOE_FF = 768
MOE_BLOCK = 128

N_GDN = (DEPTH + N_MIXERS - 1) // N_MIXERS
N_SSD = (DEPTH + N_MIXERS - 2) // N_MIXERS
N_HY = (DEPTH + N_MIXERS - 3) // N_MIXERS

kernel_name = 'hybrid_gdn_ssd_hyena_hmoe_encoder'


def layer_norm(x, g, b):
    xf = x.astype(jnp.float32)
    mu = jnp.mean(xf, axis=-1, keepdims=True)
    var = jnp.mean(jnp.square(xf - mu), axis=-1, keepdims=True)
    return ((xf - mu) * lax.rsqrt(var + LN_EPS) * g + b).astype(x.dtype)


def rms_norm(x):
    xf = x.astype(jnp.float32)
    return xf * lax.rsqrt(jnp.mean(xf * xf, axis=-1, keepdims=True) + RMS_EPS)


def l2_normalize(x):
    xf = x.astype(jnp.float32)
    return xf * lax.rsqrt(jnp.sum(xf * xf, axis=-1, keepdims=True) + 1e-6)


def centered_depthwise_conv(u, w, b=None):
    K = w.shape[0]
    half = K // 2
    L = u.shape[1]
    up = jnp.pad(u, ((0, 0), (half, half), (0, 0)))
    out = up[:, 0:L] * w[0]
    for k in range(1, K):
        out = out + up[:, k:k + L] * w[k]
    return out if b is None else out + b


def front_pad(a, n):
    return jnp.pad(a, ((0, 0), (n, 0)) + ((0, 0),) * (a.ndim - 2))


def flip_seq(a):
    return a[:, ::-1]


def chunk_gated_delta_rule(q, k, v, g, beta):
    Bsz, T, H, Dk = q.shape
    Dv = v.shape[-1]
    C = GDN_CHUNK
    N = T // C

    def to_chunks(a):
        a = jnp.moveaxis(a, 2, 1)
        return a.reshape((Bsz, H, N, C) + a.shape[3:])

    q = to_chunks(q * Dk ** -0.5)
    k = to_chunks(k)
    v = to_chunks(v)
    beta = to_chunks(beta)
    g = jnp.cumsum(to_chunks(g), axis=-1)
    idx = jnp.arange(C)
    incl = idx[:, None] >= idx[None, :]
    strict = idx[:, None] > idx[None, :]
    diff = g[..., :, None] - g[..., None, :]
    decay = jnp.where(incl, jnp.exp(jnp.where(incl, diff, 0.0)), 0.0)
    k_beta = k * beta[..., None]
    lower = jnp.where(strict, jnp.einsum('bhnid,bhnjd->bhnij', k_beta, k) * decay, 0.0)
    rhs = jnp.concatenate([v * beta[..., None], k_beta * jnp.exp(g)[..., None]], axis=-1)
    sol = lax.linalg.triangular_solve(lower + jnp.eye(C, dtype=lower.dtype), rhs,
                                      left_side=True, lower=True, unit_diagonal=True)
    u, w = sol[..., :Dv], sol[..., Dv:]
    intra = jnp.where(incl, jnp.einsum('bhnid,bhnjd->bhnij', q, k) * decay, 0.0)
    g_last = g[..., -1]
    q_dec = q * jnp.exp(g)[..., None]
    k_dec = k * jnp.exp(g_last[..., None] - g)[..., None]

    def step(S, xs):
        u_c, w_c, q_c, k_c, a_c, gl_c = xs
        v_new = u_c - jnp.einsum('bhcd,bhde->bhce', w_c, S)
        o_c = jnp.einsum('bhcd,bhde->bhce', q_c, S) + jnp.einsum('bhij,bhje->bhie', a_c, v_new)
        S = S * jnp.exp(gl_c)[..., None, None] + jnp.einsum('bhcd,bhce->bhde', k_c, v_new)
        return S, o_c

    xs = tuple(jnp.moveaxis(a, 2, 0) for a in (u, w, q_dec, k_dec, intra, g_last))
    _, o = lax.scan(step, jnp.zeros((Bsz, H, Dk, Dv), jnp.float32), xs)
    return jnp.transpose(o, (1, 0, 3, 2, 4)).reshape(Bsz, T, H, Dv)


def gdn_mixer(h, w_in, conv_w, a_log_f, a_log_b, dt_bias_f, dt_bias_b, norm_w, w_out):
    Bsz, L, _ = h.shape
    H, Hk, Dh = GDN_V_HEADS, GDN_QK_HEADS, GDN_HEAD_DIM
    proj = h @ w_in
    qkv = jax.nn.silu(centered_depthwise_conv(proj[..., :GDN_CONV_CH], conv_w))
    z = proj[..., GDN_CONV_CH:GDN_CONV_CH + GDN_V_DIM].reshape(Bsz, L, H, Dh)
    a_f, b_f, a_b, b_b = jnp.split(proj[..., GDN_CONV_CH + GDN_V_DIM:].astype(jnp.float32), 4, axis=-1)
    rep = H // Hk
    q = jnp.repeat(l2_normalize(qkv[..., :GDN_QK_DIM].reshape(Bsz, L, Hk, Dh)), rep, axis=2)
    k = jnp.repeat(l2_normalize(qkv[..., GDN_QK_DIM:2 * GDN_QK_DIM].reshape(Bsz, L, Hk, Dh)), rep, axis=2)
    v = qkv[..., 2 * GDN_QK_DIM:].reshape(Bsz, L, H, Dh).astype(jnp.float32)
    g_f = -jnp.exp(a_log_f.astype(jnp.float32)) * jax.nn.softplus(a_f + dt_bias_f)
    g_b = -jnp.exp(a_log_b.astype(jnp.float32)) * jax.nn.softplus(a_b + dt_bias_b)
    beta_f = jax.nn.sigmoid(b_f)
    beta_b = jax.nn.sigmoid(b_b)
    pad = (-N_META) % GDN_CHUNK
    q, k, v, g_f, g_b, beta_f, beta_b = [front_pad(a, pad) for a in (q, k, v, g_f, g_b, beta_f, beta_b)]
    o_f = chunk_gated_delta_rule(q, k, v, g_f, beta_f)
    o_b = flip_seq(chunk_gated_delta_rule(flip_seq(q), flip_seq(k), flip_seq(v), flip_seq(g_b), flip_seq(beta_b)))
    o = (o_f + o_b)[:, pad:]
    o = rms_norm(o) * norm_w * jax.nn.silu(z.astype(jnp.float32))
    return o.reshape(Bsz, L, GDN_V_DIM).astype(h.dtype) @ w_out


def ssd_chunked(x, dt, a, b_in, c_in):
    Bsz, T, H, P = x.shape
    G, Ns = b_in.shape[2], b_in.shape[3]
    R = H // G
    C = SSD_CHUNK
    Nc = T // C
    xdt = (x * dt[..., None]).reshape(Bsz, Nc, C, G, R, P)
    acs = jnp.cumsum((dt * a).reshape(Bsz, Nc, C, G, R), axis=2)
    acs = jnp.transpose(acs, (0, 1, 3, 4, 2))
    bc = b_in.reshape(Bsz, Nc, C, G, Ns)
    cc = c_in.reshape(Bsz, Nc, C, G, Ns)
    idx = jnp.arange(C)
    incl = idx[:, None] >= idx[None, :]
    seg = acs[..., :, None] - acs[..., None, :]
    lmat = jnp.where(incl, jnp.exp(jnp.where(incl, seg, 0.0)), 0.0)
    cb = jnp.einsum('bclgn,bcsgn->bcgls', cc, bc)
    y_diag = jnp.einsum('bcgrls,bcsgrp->bclgrp', cb[:, :, :, None] * lmat, xdt)
    decay_to_end = jnp.exp(acs[..., -1:] - acs)
    states = jnp.einsum('bclgn,bcgrl,bclgrp->bcgrpn', bc, decay_to_end, xdt)
    chunk_decay = jnp.exp(acs[..., -1])

    def step(S, xs):
        st, dec = xs
        return S * dec[..., None, None] + st, S

    _, s_in = lax.scan(step, jnp.zeros((Bsz, G, R, P, Ns), jnp.float32),
                       (jnp.moveaxis(states, 1, 0), jnp.moveaxis(chunk_decay, 1, 0)))
    s_in = jnp.moveaxis(s_in, 0, 1)
    y_off = jnp.einsum('bclgn,bcgrpn,bcgrl->bclgrp', cc, s_in, jnp.exp(acs))
    return (y_diag + y_off).reshape(Bsz, T, H, P)


def ssd_mixer(h, w_in, conv_w, conv_b, a_log_f, a_log_b, dt_bias_f, dt_bias_b, d_skip, norm_w, w_out):
    Bsz, L, _ = h.shape
    G, Ns, H, P = SSD_GROUPS, SSD_STATE, SSD_HEADS, SSD_HEAD_DIM
    proj = h @ w_in
    z = proj[..., :SSD_D_INNER]
    xbc = jax.nn.silu(centered_depthwise_conv(proj[..., SSD_D_INNER:SSD_D_INNER + SSD_CONV_CH],
                                              conv_w, conv_b)).astype(jnp.float32)
    dt_f, dt_b = jnp.split(proj[..., SSD_D_INNER + SSD_CONV_CH:].astype(jnp.float32), 2, axis=-1)
    x = xbc[..., :SSD_D_INNER].reshape(Bsz, L, H, P)
    b_in = xbc[..., SSD_D_INNER:SSD_D_INNER + G * Ns].reshape(Bsz, L, G, Ns)
    c_in = xbc[..., SSD_D_INNER + G * Ns:].reshape(Bsz, L, G, Ns)
    dt_f = jax.nn.softplus(dt_f + dt_bias_f)
    dt_b = jax.nn.softplus(dt_b + dt_bias_b)
    a_f = -jnp.exp(a_log_f.astype(jnp.float32))
    a_b = -jnp.exp(a_log_b.astype(jnp.float32))
    pad = (-N_META) % SSD_CHUNK
    xp, bp, cp, dtf, dtb = [front_pad(a, pad) for a in (x, b_in, c_in, dt_f, dt_b)]
    y_f = ssd_chunked(xp, dtf, a_f, bp, cp)
    y_b = flip_seq(ssd_chunked(flip_seq(xp), flip_seq(dtb), a_b, flip_seq(bp), flip_seq(cp)))
    y = (y_f + y_b)[:, pad:] + x * d_skip.astype(jnp.float32)[:, None]
    y = y.reshape(Bsz, L, SSD_D_INNER) * jax.nn.silu(z.astype(jnp.float32))
    y = rms_norm(y.reshape(Bsz, L, G, SSD_D_INNER // G)).reshape(Bsz, L, SSD_D_INNER) * norm_w
    return y.astype(h.dtype) @ w_out


def hyena_filters(L, fw1, fb1, freq1, fw2, fb2, freq2, fw3, fb3):
    t = jnp.linspace(0.0, 1.0, L, dtype=jnp.float32)[:, None]
    w = 2.0 * math.pi * jnp.arange(L, dtype=jnp.float32)[:, None] / L
    f = jnp.linspace(1e-4, HY_BANDS - 1, HY_BANDS, dtype=jnp.float32)[None, :]
    feats = jnp.concatenate([t, jnp.cos(f * w), -jnp.sin(f * w)], axis=-1)
    z = jnp.sin(freq1 * (feats @ fw1 + fb1))
    z = jnp.sin(freq2 * (z @ fw2 + fb2))
    resp = (z @ fw3 + fb3).astype(jnp.float32).reshape(L, HY_DIRS, HY_ORDER, HY_WIDTH)
    deltas = jnp.abs(jnp.linspace(math.log(HY_TARGET) / HY_DECAY_PCT_SHORT,
                                  math.log(HY_TARGET) / HY_DECAY_PCT_LONG, HY_WIDTH, dtype=jnp.float32))
    resp = resp * jnp.exp(-t * deltas)[:, None, None, :]
    fwd, bwd = resp[:, 0], resp[:, 1]
    return jnp.concatenate([fwd, jnp.zeros_like(fwd[:1]), bwd[:0:-1]], axis=0)


def fft_long_conv(u, resp, skip):
    L = u.shape[1]
    n = 2 * L
    spec = jnp.fft.rfft(u, n=n, axis=1) * jnp.fft.rfft(resp, n=n, axis=0)[None]
    return jnp.fft.irfft(spec, n=n, axis=1)[:, :L] + u * skip


def hyena_mixer(h, w_in, conv_w, conv_b, fw1, fb1, freq1, fw2, fb2, freq2, fw3, fb3, filt_skip, w_out):
    L = h.shape[1]
    u = centered_depthwise_conv(h @ w_in, conv_w, conv_b).astype(jnp.float32)
    x1, x2, v = jnp.split(u, 3, axis=-1)
    resp = hyena_filters(L, fw1, fb1, freq1, fw2, fb2, freq2, fw3, fb3)
    gates = (x1, x2)
    z = v
    for o in range(HY_ORDER):
        z = gates[o] * fft_long_conv(z, resp[:, o], filt_skip[o].astype(jnp.float32))
    return z.astype(h.dtype) @ w_out


def hier_moe(h, wr_group, br_group, wr_expert, br_expert, w_gate, w_up, w_down):
    Bsz, L, D = h.shape
    xt = h.reshape(-1, D)
    T = xt.shape[0]
    glog = (xt @ wr_group).astype(jnp.float32) + br_group
    gsel = jnp.argmax(glog, axis=-1)
    gw = jnp.take_along_axis(jax.nn.softmax(glog, axis=-1), gsel[:, None], axis=-1)
    elog = ((xt @ wr_expert).astype(jnp.float32) + br_expert).reshape(T, MOE_GROUPS, MOE_PER_GROUP)
    elog = jnp.take_along_axis(elog, gsel[:, None, None], axis=1)[:, 0]
    top_v, top_i = lax.top_k(elog, MOE_TOPK)
    ew = jax.nn.softmax(top_v, axis=-1) * gw
    eid = gsel[:, None] * MOE_PER_GROUP + top_i
    A = T * MOE_TOPK
    flat_e = eid.reshape(-1)
    flat_w = ew.reshape(-1)
    flat_t = jnp.repeat(jnp.arange(T), MOE_TOPK)
    order = jnp.argsort(flat_e)
    se, st, sw = flat_e[order], flat_t[order], flat_w[order]
    counts = jnp.bincount(flat_e, length=MOE_EXPERTS)
    padded = (counts + MOE_BLOCK - 1) // MOE_BLOCK * MOE_BLOCK
    pend = jnp.cumsum(padded)
    pstart = pend - padded
    start = jnp.cumsum(counts) - counts
    dest = pstart[se] + jnp.arange(A) - start[se]
    n_blocks = -(-A // MOE_BLOCK) + MOE_EXPERTS
    slot_tok = jnp.full((n_blocks * MOE_BLOCK,), T, jnp.int32).at[dest].set(st.astype(jnp.int32))
    slot_w = jnp.zeros((n_blocks * MOE_BLOCK,), jnp.float32).at[dest].set(sw)
    blk_expert = jnp.minimum(jnp.searchsorted(pend, jnp.arange(n_blocks) * MOE_BLOCK, side='right'),
                             MOE_EXPERTS - 1)
    x_pad = jnp.concatenate([xt, jnp.zeros((1, D), xt.dtype)], axis=0)

    def expert_block(args):
        tok, e = args
        xb = x_pad[tok]
        hid = jax.nn.silu(xb @ w_gate[e]) * (xb @ w_up[e])
        return hid @ w_down[e]

    yb = lax.map(expert_block, (slot_tok.reshape(n_blocks, MOE_BLOCK), blk_expert))
    y = jnp.zeros((T + 1, D), jnp.float32).at[slot_tok].add(
        yb.reshape(-1, D).astype(jnp.float32) * slot_w[:, None])
    return y[:T].astype(h.dtype).reshape(Bsz, L, D)


def setup_inputs(seed: int = 0) -> dict:
    key = jax.random.key(seed)
    keys = iter(jax.random.split(key, 64))

    def normal(shape, scale):
        return jax.random.normal(next(keys), shape, jnp.float32) * scale

    def uniform(shape, lo, hi):
        return jax.random.uniform(next(keys), shape, jnp.float32, lo, hi)

    def dt_bias(shape):
        dt = jnp.exp(uniform(shape, math.log(1e-3), math.log(1e-1)))
        return dt + jnp.log(-jnp.expm1(-dt))

    def a_log(shape):
        return jnp.log(uniform(shape, 1.0, 16.0))

    D = D_MODEL
    L = N_META + SEQ
    HID = HY_FILTER_HIDDEN
    HOUT = HY_DIRS * HY_ORDER * HY_WIDTH
    return {
        'x': normal((BATCH, SEQ, D), 1.0),
        'meta_tokens': normal((N_META, D), 1.0),
        'ln_mix_g': 1.0 + normal((DEPTH, D), 0.02),
        'ln_mix_b': normal((DEPTH, D), 0.02),
        'ln_ffn_g': 1.0 + normal((DEPTH, D), 0.02),
        'ln_ffn_b': normal((DEPTH, D), 0.02),
        'gdn_w_in': normal((N_GDN, D, GDN_IN), D ** -0.5),
        'gdn_conv_w': normal((N_GDN, SHORT_CONV, GDN_CONV_CH), SHORT_CONV ** -0.5),
        'gdn_a_log_f': a_log((N_GDN, GDN_V_HEADS)),
        'gdn_a_log_b': a_log((N_GDN, GDN_V_HEADS)),
        'gdn_dt_bias_f': dt_bias((N_GDN, GDN_V_HEADS)),
        'gdn_dt_bias_b': dt_bias((N_GDN, GDN_V_HEADS)),
        'gdn_norm_w': 1.0 + normal((N_GDN, GDN_HEAD_DIM), 0.02),
        'gdn_w_out': normal((N_GDN, GDN_V_DIM, D), GDN_V_DIM ** -0.5 * DN_BETA),
        'ssd_w_in': normal((N_SSD, D, SSD_IN), D ** -0.5),
        'ssd_conv_w': normal((N_SSD, SHORT_CONV, SSD_CONV_CH), SHORT_CONV ** -0.5),
        'ssd_conv_b': normal((N_SSD, SSD_CONV_CH), 0.02),
        'ssd_a_log_f': a_log((N_SSD, SSD_HEADS)),
        'ssd_a_log_b': a_log((N_SSD, SSD_HEADS)),
        'ssd_dt_bias_f': dt_bias((N_SSD, SSD_HEADS)),
        'ssd_dt_bias_b': dt_bias((N_SSD, SSD_HEADS)),
        'ssd_d_skip': 1.0 + normal((N_SSD, SSD_HEADS), 0.02),
        'ssd_norm_w': 1.0 + normal((N_SSD, SSD_D_INNER), 0.02),
        'ssd_w_out': normal((N_SSD, SSD_D_INNER, D), SSD_D_INNER ** -0.5 * DN_BETA),
        'hy_w_in': normal((N_HY, D, 3 * HY_WIDTH), D ** -0.5),
        'hy_conv_w': normal((N_HY, HY_SHORT, 3 * HY_WIDTH), HY_SHORT ** -0.5),
        'hy_conv_b': normal((N_HY, 3 * HY_WIDTH), 0.02),
        'hy_fw1': normal((N_HY, HY_EMB, HID), HY_EMB ** -0.5),
        'hy_fb1': normal((N_HY, HID), 0.02),
        'hy_freq1': 1.0 + normal((N_HY, HID), 0.02),
        'hy_fw2': normal((N_HY, HID, HID), HID ** -0.5),
        'hy_fb2': normal((N_HY, HID), 0.02),
        'hy_freq2': 1.0 + normal((N_HY, HID), 0.02),
        'hy_fw3': normal((N_HY, HID, HOUT), (HID * L) ** -0.5),
        'hy_fb3': normal((N_HY, HOUT), L ** -0.5),
        'hy_filt_skip': normal((N_HY, HY_ORDER, HY_WIDTH), 1.0),
        'hy_w_out': normal((N_HY, HY_WIDTH, D), HY_WIDTH ** -0.5 * DN_BETA),
        'moe_wr_group': normal((DEPTH, D, MOE_GROUPS), D ** -0.5),
        'moe_br_group': normal((DEPTH, MOE_GROUPS), 0.01),
        'moe_wr_expert': normal((DEPTH, D, MOE_EXPERTS), D ** -0.5),
        'moe_br_expert': normal((DEPTH, MOE_EXPERTS), 0.01),
        'moe_w_gate': normal((DEPTH, MOE_EXPERTS, D, MOE_FF), D ** -0.5),
        'moe_w_up': normal((DEPTH, MOE_EXPERTS, D, MOE_FF), D ** -0.5),
        'moe_w_down': normal((DEPTH, MOE_EXPERTS, MOE_FF, D), MOE_FF ** -0.5 * DN_BETA),
    }


def reference(x, meta_tokens, ln_mix_g, ln_mix_b, ln_ffn_g, ln_ffn_b,
              gdn_w_in, gdn_conv_w, gdn_a_log_f, gdn_a_log_b, gdn_dt_bias_f, gdn_dt_bias_b,
              gdn_norm_w, gdn_w_out,
              ssd_w_in, ssd_conv_w, ssd_conv_b, ssd_a_log_f, ssd_a_log_b, ssd_dt_bias_f,
              ssd_dt_bias_b, ssd_d_skip, ssd_norm_w, ssd_w_out,
              hy_w_in, hy_conv_w, hy_conv_b, hy_fw1, hy_fb1, hy_freq1, hy_fw2, hy_fb2, hy_freq2,
              hy_fw3, hy_fb3, hy_filt_skip, hy_w_out,
              moe_wr_group, moe_br_group, moe_wr_expert, moe_br_expert, moe_w_gate, moe_w_up,
              moe_w_down):
    Bsz = x.shape[0]
    meta = jnp.broadcast_to(meta_tokens[None].astype(x.dtype), (Bsz, N_META, D_MODEL))
    h = jnp.concatenate([meta, x], axis=1)
    for i in range(DEPTH):
        kind, j = i % N_MIXERS, i // N_MIXERS
        if kind == 0:
            m = gdn_mixer(h, gdn_w_in[j], gdn_conv_w[j], gdn_a_log_f[j], gdn_a_log_b[j],
                          gdn_dt_bias_f[j], gdn_dt_bias_b[j], gdn_norm_w[j], gdn_w_out[j])
        elif kind == 1:
            m = ssd_mixer(h, ssd_w_in[j], ssd_conv_w[j], ssd_conv_b[j], ssd_a_log_f[j], ssd_a_log_b[j],
                          ssd_dt_bias_f[j], ssd_dt_bias_b[j], ssd_d_skip[j], ssd_norm_w[j], ssd_w_out[j])
        else:
            m = hyena_mixer(h, hy_w_in[j], hy_conv_w[j], hy_conv_b[j], hy_fw1[j], hy_fb1[j], hy_freq1[j],
                            hy_fw2[j], hy_fb2[j], hy_freq2[j], hy_fw3[j], hy_fb3[j], hy_filt_skip[j],
                            hy_w_out[j])
        h = layer_norm(DN_ALPHA * h + m, ln_mix_g[i], ln_mix_b[i])
        f = hier_moe(h, moe_wr_group[i], moe_br_group[i], moe_wr_expert[i], moe_br_expert[i],
                     moe_w_gate[i], moe_w_up[i], moe_w_down[i])
        h = layer_norm(DN_ALPHA * h + f, ln_ffn_g[i], ln_ffn_b[i])
    return h[:, N_META:]
```

```python
import functools
import math

import numpy as np
import jax
import jax.numpy as jnp
from jax import lax
from jax.experimental import pallas as pl
from jax.experimental.pallas import tpu as pltpu

F32 = jnp.float32
BF16 = jnp.bfloat16
HIGHEST = lax.Precision.HIGHEST

LANES = 128
VMEM_LIMIT = 56 * 1024 * 1024

DEPTH = 4
N_MIXERS = 3
DN_ALPHA = (2.0 * DEPTH) ** 0.25
LN_EPS = 1e-5
RMS_EPS = 1e-6

GDN_QK_HEADS = 16
GDN_V_HEADS = 32
GDN_HEAD_DIM = 128
GDN_QK_DIM = GDN_QK_HEADS * GDN_HEAD_DIM
GDN_V_DIM = GDN_V_HEADS * GDN_HEAD_DIM
GDN_CONV_CH = 2 * GDN_QK_DIM + GDN_V_DIM
GDN_CHUNK = 64

SSD_HEAD_DIM = 64
SSD_GROUPS = 8
SSD_STATE = 128
SSD_CHUNK = 128

HY_BANDS = 16
HY_TARGET = 1e-2
HY_DECAY_PCT_SHORT = 0.3
HY_DECAY_PCT_LONG = 1.5

MOE_GROUPS = 4
MOE_PER_GROUP = 8
MOE_EXPERTS = MOE_GROUPS * MOE_PER_GROUP
MOE_BLOCK = 128


def _cparams(*sem):
    return pltpu.CompilerParams(dimension_semantics=sem, vmem_limit_bytes=VMEM_LIMIT)


def _pick(n, candidates):
    for c in candidates:
        if n % c == 0:
            return c
    raise ValueError(f"no tile in {candidates} divides {n}")


def _row_ids(i, tr):
    return i * tr + lax.broadcasted_iota(jnp.int32, (tr, 1), 0)


def _layer_norm_rows(y, g, b):
    mu = jnp.mean(y, axis=-1, keepdims=True)
    d = y - mu
    var = jnp.mean(d * d, axis=-1, keepdims=True)
    return d * lax.rsqrt(var + LN_EPS) * g + b


def _softplus(x):
    return jnp.maximum(x, 0.0) + jnp.log(1.0 + jnp.exp(-jnp.abs(x)))


def _sigmoid(x):
    return 1.0 / (1.0 + jnp.exp(-x))


def _proj_body(a_ref, w_ref, o_ref):
    o_ref[...] = jnp.dot(a_ref[...], w_ref[...].astype(BF16), preferred_element_type=F32)


def _proj(a, w, col0, ncols):
    Lp, K = a.shape
    tn = _pick(ncols, (1024, 512, 256, 128))
    assert col0 % tn == 0
    tm = _pick(Lp, (1408, 1056, 768, 512, 256))
    return pl.pallas_call(
        _proj_body,
        grid=(ncols // tn, Lp // tm),
        in_specs=[pl.BlockSpec((tm, K), lambda j, i: (i, 0)),
                  pl.BlockSpec((K, tn), lambda j, i: (0, col0 // tn + j))],
        out_specs=pl.BlockSpec((tm, tn), lambda j, i: (i, j)),
        out_shape=jax.ShapeDtypeStruct((Lp, ncols), F32),
        compiler_params=_cparams("parallel", "parallel"),
        name="proj",
    )(a, w)


def _out_ln_body(a_ref, w_ref, h_ref, g_ref, b_ref, o_ref, ob_ref, acc_ref, *, nk):
    k = pl.program_id(1)

    @pl.when(k == 0)
    def _():
        acc_ref[...] = jnp.zeros_like(acc_ref)

    acc_ref[...] += jnp.dot(a_ref[...], w_ref[...].astype(BF16), preferred_element_type=F32)

    @pl.when(k == nk - 1)
    def _():
        y = _layer_norm_rows(DN_ALPHA * h_ref[...] + acc_ref[...], g_ref[...], b_ref[...])
        o_ref[...] = y
        ob_ref[...] = y.astype(BF16)


def _out_ln(a, w, h, g, b):
    Lp, K = a.shape
    D = w.shape[1]
    tm = _pick(Lp, (704, 512, 256))
    tk = _pick(K, (512, 256))
    nk = K // tk
    return pl.pallas_call(
        functools.partial(_out_ln_body, nk=nk),
        grid=(Lp // tm, nk),
        in_specs=[pl.BlockSpec((tm, tk), lambda i, k: (i, k)),
                  pl.BlockSpec((tk, D), lambda i, k: (k, 0)),
                  pl.BlockSpec((tm, D), lambda i, k: (i, 0)),
                  pl.BlockSpec((1, D), lambda i, k: (0, 0)),
                  pl.BlockSpec((1, D), lambda i, k: (0, 0))],
        out_specs=[pl.BlockSpec((tm, D), lambda i, k: (i, 0)),
                   pl.BlockSpec((tm, D), lambda i, k: (i, 0))],
        out_shape=[jax.ShapeDtypeStruct((Lp, D), F32), jax.ShapeDtypeStruct((Lp, D), BF16)],
        scratch_shapes=[pltpu.VMEM((tm, D), F32)],
        compiler_params=_cparams("parallel", "arbitrary"),
        name="out_ln",
    )(a, w, h, g.reshape(1, D), b.reshape(1, D))


def _conv_body(prev_ref, cur_ref, next_ref, w_ref, b_ref, o_ref, *, taps, L, tr, nblk, silu, l2norm):
    i = pl.program_id(0)
    half = taps // 2
    r8 = lax.broadcasted_iota(jnp.int32, (8, 1), 0)
    cur = jnp.where(_row_ids(i, tr) < L, cur_ref[...], 0.0)
    prev = jnp.where(jnp.logical_and(i > 0, i * tr - 8 + r8 < L), prev_ref[...], 0.0)
    nxt = jnp.where(jnp.logical_and(i < nblk - 1, (i + 1) * tr + r8 < L), next_ref[...], 0.0)
    ext = jnp.concatenate([prev, cur, nxt], axis=0)
    acc = ext[8 - half:8 - half + tr] * w_ref[0:1, :]
    for k in range(1, taps):
        s = 8 + k - half
        acc = acc + ext[s:s + tr] * w_ref[k:k + 1, :]
    acc = acc + b_ref[...]
    if silu:
        acc = acc * _sigmoid(acc)
    acc = jnp.where(_row_ids(i, tr) < L, acc, 0.0)
    if l2norm:
        tc = acc.shape[1]
        for hh in range(tc // LANES):
            blk = acc[:, hh * LANES:(hh + 1) * LANES]
            ss = jnp.sum(blk * blk, axis=-1, keepdims=True)
            o_ref[:, hh * LANES:(hh + 1) * LANES] = blk * lax.rsqrt(ss + 1e-6)
    else:
        o_ref[...] = acc


def _conv(pre, w, b, *, L, col0, ncols, silu, l2norm):
    Lp = pre.shape[0]
    taps = w.shape[0]
    tr = _pick(Lp, (768, 512, 256))
    tc = _pick(ncols, (512, 256, 128))
    assert col0 % tc == 0
    nblk = Lp // tr
    c0 = col0 // tc
    r8 = tr // 8
    last8 = Lp // 8 - 1
    return pl.pallas_call(
        functools.partial(_conv_body, taps=taps, L=L, tr=tr, nblk=nblk, silu=silu, l2norm=l2norm),
        grid=(nblk, ncols // tc),
        in_specs=[pl.BlockSpec((8, tc), lambda i, j: (jnp.maximum(i * r8 - 1, 0), c0 + j)),
                  pl.BlockSpec((tr, tc), lambda i, j: (i, c0 + j)),
                  pl.BlockSpec((8, tc), lambda i, j: (jnp.minimum((i + 1) * r8, last8), c0 + j)),
                  pl.BlockSpec((taps, tc), lambda i, j: (0, c0 + j)),
                  pl.BlockSpec((1, tc), lambda i, j: (0, c0 + j))],
        out_specs=pl.BlockSpec((tr, tc), lambda i, j: (i, j)),
        out_shape=jax.ShapeDtypeStruct((Lp, ncols), F32),
        compiler_params=_cparams("parallel", "parallel"),
        name="conv",
    )(pre, pre, pre, w, b)


def _tri(n, upper):
    r = lax.broadcasted_iota(jnp.int32, (n, n), 0)
    c = lax.broadcasted_iota(jnp.int32, (n, n), 1)
    return ((r <= c) if upper else (r >= c)).astype(F32)


def _gdn_gates_body(raw_ref, alog_ref, dtb_ref, o_ref, *, L, tr):
    i = pl.program_id(0)
    nh = GDN_V_HEADS
    lane = lax.broadcasted_iota(jnp.int32, (1, 4 * nh), 1)
    is_g = (lane // nh) % 2 == 0
    raw = raw_ref[...]
    g = -jnp.exp(alog_ref[...]) * _softplus(raw + dtb_ref[...])
    val = jnp.where(is_g, g, _sigmoid(raw))
    val = jnp.where(_row_ids(i, tr) < L, val, 0.0)
    C = GDN_CHUNK
    lo, up = _tri(C, False), _tri(C, True)
    for c in range(tr // C):
        blk = val[c * C:(c + 1) * C]
        pre = jnp.dot(lo, blk, precision=HIGHEST, preferred_element_type=F32)
        suf = jnp.dot(up, blk, precision=HIGHEST, preferred_element_type=F32)
        o_ref[c * C:(c + 1) * C, :] = jnp.where(lane < nh, pre, jnp.where(lane // nh == 2, suf, blk))


def _gdn_gates(raw, a_log_f, a_log_b, dt_bias_f, dt_bias_b, *, L):
    Lp, W = raw.shape
    zeros = jnp.zeros_like(a_log_f)
    alog = jnp.concatenate([a_log_f, zeros, a_log_b, zeros]).reshape(1, W)
    dtb = jnp.concatenate([dt_bias_f, zeros, dt_bias_b, zeros]).reshape(1, W)
    tr = _pick(Lp, (256,))
    return pl.pallas_call(
        functools.partial(_gdn_gates_body, L=L, tr=tr),
        grid=(Lp // tr,),
        in_specs=[pl.BlockSpec((tr, W), lambda i: (i, 0)),
                  pl.BlockSpec((1, W), lambda i: (0, 0)),
                  pl.BlockSpec((1, W), lambda i: (0, 0))],
        out_specs=pl.BlockSpec((tr, W), lambda i: (i, 0)),
        out_shape=jax.ShapeDtypeStruct((Lp, W), F32),
        compiler_params=_cparams("parallel"),
        name="gdn_gates",
    )(raw, alog, dtb)


def _gdn_chunk_body(q_ref, k_ref, v_ref, gcol_ref, grow_ref, o_ref, s_ref, *, reverse, nchunks):
    i = pl.program_id(1)

    @pl.when(i == 0)
    def _():
        s_ref[...] = jnp.zeros_like(s_ref)

    C = GDN_CHUNK
    Dh = GDN_HEAD_DIM
    rr = lax.broadcasted_iota(jnp.int32, (C, C), 0)
    cc = lax.broadcasted_iota(jnp.int32, (C, C), 1)
    incl = (rr <= cc) if reverse else (rr >= cc)
    strict = (rr < cc) if reverse else (rr > cc)
    eye = (rr == cc).astype(F32)
    scale = Dh ** -0.5
    last = 0 if reverse else C - 1
    order = range(nchunks - 1, -1, -1) if reverse else range(nchunks)
    states = [s_ref[0], s_ref[1]]
    for c in order:
        sl = slice(c * C, (c + 1) * C)
        kc = k_ref[sl, :]
        qc = q_ref[sl, :] * scale
        for r in range(2):
            vc = v_ref[sl, r * Dh:(r + 1) * Dh]
            g_col = gcol_ref[sl, r:r + 1]
            b_col = gcol_ref[sl, 2 + r:3 + r]
            g_row = grow_ref[r:r + 1, sl]
            g_last = g_row[:, last:last + 1]
            decay = jnp.where(incl, jnp.exp(jnp.where(incl, g_col - g_row, 0.0)), 0.0)
            kb = kc * b_col
            eg = jnp.exp(g_col)
            kq = jnp.concatenate([kb, qc], axis=0)
            kk = lax.dot_general(kq, kc, (((1,), (1,)), ((), ())), preferred_element_type=F32)
            p = -jnp.where(strict, kk[:C] * decay, 0.0)
            intra = jnp.where(incl, kk[C:] * decay, 0.0)
            t = eye + p
            pk = jnp.dot(p, p, preferred_element_type=F32)
            steps = int(math.log2(C)) - 1
            for lvl in range(steps):
                if lvl < steps - 1:
                    pp = jnp.dot(pk, jnp.concatenate([pk, t], axis=1), preferred_element_type=F32)
                    pk = pp[:, :C]
                    t = t + pp[:, C:]
                else:
                    t = t + jnp.dot(pk, t, preferred_element_type=F32)
            rhs = jnp.concatenate([vc * b_col, kb * eg], axis=1)
            uw = jnp.dot(t, rhs, preferred_element_type=F32)
            u, w = uw[:, :Dh], uw[:, Dh:]
            q_dec = qc * eg
            k_dec = kc * jnp.exp(g_last - g_col)
            S = states[r]
            ws = jnp.dot(jnp.concatenate([w, q_dec], axis=0), S, preferred_element_type=F32)
            v_new = u - ws[:C]
            o = ws[C:] + jnp.dot(intra, v_new, preferred_element_type=F32)
            o_ref[sl, r * Dh:(r + 1) * Dh] = o
            states[r] = S * jnp.exp(g_last) + lax.dot_general(
                k_dec, v_new, (((0,), (0,)), ((), ())), preferred_element_type=F32)
    s_ref[0] = states[0]
    s_ref[1] = states[1]


def _gdn_chunk(q, k, v, gcol, grow, *, direction):
    Lp = q.shape[0]
    Dh = GDN_HEAD_DIM
    tb = _pick(Lp, (256,))
    nblk = Lp // tb
    reverse = direction == 1

    def blk(i):
        return nblk - 1 - i if reverse else i

    return pl.pallas_call(
        functools.partial(_gdn_chunk_body, reverse=reverse, nchunks=tb // GDN_CHUNK),
        grid=(GDN_QK_HEADS, nblk),
        in_specs=[pl.BlockSpec((tb, Dh), lambda h, i: (blk(i), h)),
                  pl.BlockSpec((tb, Dh), lambda h, i: (blk(i), h)),
                  pl.BlockSpec((tb, 2 * Dh), lambda h, i: (blk(i), h)),
                  pl.BlockSpec((None, None, tb, 4), lambda h, i: (direction, h, blk(i), 0)),
                  pl.BlockSpec((None, None, 2, tb), lambda h, i: (direction, h, 0, blk(i)))],
        out_specs=pl.BlockSpec((tb, 2 * Dh), lambda h, i: (blk(i), h)),
        out_shape=jax.ShapeDtypeStruct((Lp, GDN_V_DIM), F32),
        scratch_shapes=[pltpu.VMEM((2, Dh, Dh), F32)],
        compiler_params=_cparams("parallel", "arbitrary"),
        name="gdn_chunk_bwd" if reverse else "gdn_chunk_fwd",
    )(q, k, v, gcol, grow)


def _gdn_norm_body(of_ref, ob_ref, z_ref, nw_ref, o_ref):
    tc = of_ref.shape[1]
    for hh in range(tc // LANES):
        sl = slice(hh * LANES, (hh + 1) * LANES)
        o = of_ref[:, sl] + ob_ref[:, sl]
        o = o * lax.rsqrt(jnp.mean(o * o, axis=-1, keepdims=True) + RMS_EPS)
        z = z_ref[:, sl]
        o_ref[:, sl] = (o * nw_ref[...] * (z * _sigmoid(z))).astype(BF16)


def _gdn_norm(o_f, o_b, z, norm_w):
    Lp, W = o_f.shape
    tr = _pick(Lp, (768, 512, 256))
    tc = 512
    spec = pl.BlockSpec((tr, tc), lambda i, j: (i, j))
    return pl.pallas_call(
        _gdn_norm_body,
        grid=(Lp // tr, W // tc),
        in_specs=[spec, spec, spec, pl.BlockSpec((1, LANES), lambda i, j: (0, 0))],
        out_specs=spec,
        out_shape=jax.ShapeDtypeStruct((Lp, W), BF16),
        compiler_params=_cparams("parallel", "parallel"),
        name="gdn_norm",
    )(o_f, o_b, z, norm_w.reshape(1, LANES))


def _gdn_mixer(hb, h, L, w_in, conv_w, a_log_f, a_log_b, dt_bias_f, dt_bias_b, norm_w, w_out, ln_g, ln_b):
    Lp = hb.shape[0]
    qkv_pre = _proj(hb, w_in, 0, GDN_CONV_CH)
    z = _proj(hb, w_in, GDN_CONV_CH, GDN_V_DIM)
    graw = _proj(hb, w_in, GDN_CONV_CH + GDN_V_DIM, 4 * GDN_V_HEADS)
    zero_b = jnp.zeros((1, GDN_CONV_CH), F32)
    qk = _conv(qkv_pre, conv_w, zero_b, L=L, col0=0, ncols=2 * GDN_QK_DIM, silu=True, l2norm=True)
    v = _conv(qkv_pre, conv_w, zero_b, L=L, col0=2 * GDN_QK_DIM, ncols=GDN_V_DIM, silu=True, l2norm=False)
    gpk = _gdn_gates(graw, a_log_f, a_log_b, dt_bias_f, dt_bias_b, L=L)
    g5 = gpk.reshape(Lp, 2, 2, GDN_QK_HEADS, 2)
    gcol = jnp.transpose(g5, (1, 3, 0, 2, 4)).reshape(2, GDN_QK_HEADS, Lp, 4)
    grow = jnp.transpose(g5[:, :, 0], (1, 2, 3, 0))
    q = qk[:, :GDN_QK_DIM]
    k = qk[:, GDN_QK_DIM:]
    o_f = _gdn_chunk(q, k, v, gcol, grow, direction=0)
    o_b = _gdn_chunk(q, k, v, gcol, grow, direction=1)
    a = _gdn_norm(o_f, o_b, z, norm_w)
    return _out_ln(a, w_out, h, ln_g, ln_b)


def _ssd_dt_body(raw_ref, alog_ref, dtb_ref, dt_ref, acs_ref, *, L, tr, nheads):
    i = pl.program_id(0)
    lane = lax.broadcasted_iota(jnp.int32, (1, 2 * nheads), 1)
    dt = _softplus(raw_ref[...] + dtb_ref[...])
    dt = jnp.where(_row_ids(i, tr) < L, dt, 0.0)
    dt_ref[...] = dt
    da = dt * (-jnp.exp(alog_ref[...]))
    C = SSD_CHUNK
    lo, up = _tri(C, False), _tri(C, True)
    for c in range(tr // C):
        blk = da[c * C:(c + 1) * C]
        pre = jnp.dot(lo, blk, precision=HIGHEST, preferred_element_type=F32)
        suf = jnp.dot(up, blk, precision=HIGHEST, preferred_element_type=F32)
        acs_ref[c * C:(c + 1) * C, :] = jnp.where(lane < nheads, pre, suf)


def _ssd_dt(raw, a_log_f, a_log_b, dt_bias_f, dt_bias_b, *, L):
    Lp, W = raw.shape
    alog = jnp.concatenate([a_log_f, a_log_b]).reshape(1, W)
    dtb = jnp.concatenate([dt_bias_f, dt_bias_b]).reshape(1, W)
    tr = _pick(Lp, (256,))
    spec = pl.BlockSpec((tr, W), lambda i: (i, 0))
    par = pl.BlockSpec((1, W), lambda i: (0, 0))
    return pl.pallas_call(
        functools.partial(_ssd_dt_body, L=L, tr=tr, nheads=W // 2),
        grid=(Lp // tr,),
        in_specs=[spec, par, par],
        out_specs=[spec, spec],
        out_shape=[jax.ShapeDtypeStruct((Lp, W), F32)] * 2,
        compiler_params=_cparams("parallel"),
        name="ssd_dt",
    )(raw, alog, dtb)


def _expand_heads(col, width):
    rows, R = col.shape
    return jnp.concatenate([jnp.broadcast_to(col[:, r:r + 1], (rows, width)) for r in range(R)], axis=1)


def _ssd_chunk_body(x_ref, b_ref, c_ref, scol_ref, srow_ref, o_ref, s_ref, *, reverse, nchunks, R):
    i = pl.program_id(1)

    @pl.when(i == 0)
    def _():
        s_ref[...] = jnp.zeros_like(s_ref)

    C = SSD_CHUNK
    P = SSD_HEAD_DIM
    rr = lax.broadcasted_iota(jnp.int32, (C, C), 0)
    cc = lax.broadcasted_iota(jnp.int32, (C, C), 1)
    incl = (rr <= cc) if reverse else (rr >= cc)
    last = 0 if reverse else C - 1
    order = range(nchunks - 1, -1, -1) if reverse else range(nchunks)
    S = s_ref[...]
    for c in order:
        sl = slice(c * C, (c + 1) * C)
        x = x_ref[sl, :]
        bm = b_ref[sl, :]
        cm = c_ref[sl, :]
        dt_col = scol_ref[sl, 0:R]
        a_col = scol_ref[sl, R:2 * R]
        a_row = srow_ref[:, sl]
        a_last_col = a_col[last:last + 1, :]
        cb = lax.dot_general(cm, bm, (((1,), (1,)), ((), ())), preferred_element_type=F32)
        xdt = x * _expand_heads(dt_col, P)
        for r in range(R):
            seg = a_col[:, r:r + 1] - a_row[r:r + 1, :]
            lmat = jnp.where(incl, jnp.exp(jnp.where(incl, seg, 0.0)), 0.0)
            o_ref[sl, r * P:(r + 1) * P] = jnp.dot(cb * lmat, xdt[:, r * P:(r + 1) * P],
                                                   preferred_element_type=F32)
        y_off = jnp.dot(cm, S, preferred_element_type=F32) * _expand_heads(jnp.exp(a_col), P)
        o_ref[sl, :] += y_off
        xs = xdt * _expand_heads(jnp.exp(a_last_col - a_col), P)
        st = lax.dot_general(bm, xs, (((0,), (0,)), ((), ())), preferred_element_type=F32)
        S = S * _expand_heads(jnp.exp(a_last_col), P) + st
    s_ref[...] = S


def _ssd_chunk(xbc, scol, srow, *, direction, d_inner):
    Lp = xbc.shape[0]
    G, N, P = SSD_GROUPS, SSD_STATE, SSD_HEAD_DIM
    R = d_inner // P // G
    gw = R * P
    tb = _pick(Lp, (256,))
    nblk = Lp // tb
    reverse = direction == 1
    b0 = d_inner // N
    c0 = b0 + G

    def blk(i):
        return nblk - 1 - i if reverse else i

    return pl.pallas_call(
        functools.partial(_ssd_chunk_body, reverse=reverse, nchunks=tb // SSD_CHUNK, R=R),
        grid=(G, nblk),
        in_specs=[pl.BlockSpec((tb, gw), lambda g, i: (blk(i), g)),
                  pl.BlockSpec((tb, N), lambda g, i: (blk(i), b0 + g)),
                  pl.BlockSpec((tb, N), lambda g, i: (blk(i), c0 + g)),
                  pl.BlockSpec((None, None, tb, 2 * R), lambda g, i: (direction, g, blk(i), 0)),
                  pl.BlockSpec((None, None, R, tb), lambda g, i: (direction, g, 0, blk(i)))],
        out_specs=pl.BlockSpec((tb, gw), lambda g, i: (blk(i), g)),
        out_shape=jax.ShapeDtypeStruct((Lp, d_inner), F32),
        scratch_shapes=[pltpu.VMEM((N, gw), F32)],
        compiler_params=_cparams("parallel", "arbitrary"),
        name="ssd_chunk_bwd" if reverse else "ssd_chunk_fwd",
    )(xbc, xbc, xbc, scol, srow)


def _ssd_norm_body(yf_ref, yb_ref, x_ref, z_ref, dsk_ref, nw_ref, o_ref):
    z = z_ref[...]
    y = (yf_ref[...] + yb_ref[...] + x_ref[...] * dsk_ref[...]) * (z * _sigmoid(z))
    y = y * lax.rsqrt(jnp.mean(y * y, axis=-1, keepdims=True) + RMS_EPS)
    o_ref[...] = (y * nw_ref[...]).astype(BF16)


def _ssd_norm(y_f, y_b, xbc, z, d_skip_row, norm_w):
    Lp, W = y_f.shape
    gw = W // SSD_GROUPS
    tr = _pick(Lp, (768, 512, 256))
    spec = pl.BlockSpec((tr, gw), lambda i, j: (i, j))
    par = pl.BlockSpec((1, gw), lambda i, j: (0, j))
    return pl.pallas_call(
        _ssd_norm_body,
        grid=(Lp // tr, SSD_GROUPS),
        in_specs=[spec, spec, spec, spec, par, par],
        out_specs=spec,
        out_shape=jax.ShapeDtypeStruct((Lp, W), BF16),
        compiler_params=_cparams("parallel", "parallel"),
        name="ssd_norm",
    )(y_f, y_b, xbc, z, d_skip_row, norm_w.reshape(1, W))


def _ssd_mixer(hb, h, L, w_in, conv_w, conv_b, a_log_f, a_log_b, dt_bias_f, dt_bias_b, d_skip, norm_w,
               w_out, ln_g, ln_b):
    Lp = hb.shape[0]
    d_inner = w_out.shape[0]
    conv_ch = conv_w.shape[1]
    nheads = d_inner // SSD_HEAD_DIM
    G = SSD_GROUPS
    R = nheads // G
    z = _proj(hb, w_in, 0, d_inner)
    xbc_pre = _proj(hb, w_in, d_inner, conv_ch)
    dt_raw = _proj(hb, w_in, d_inner + conv_ch, 2 * nheads)
    xbc = _conv(xbc_pre, conv_w, conv_b.reshape(1, conv_ch), L=L, col0=0, ncols=conv_ch, silu=True, l2norm=False)
    dt, acs = _ssd_dt(dt_raw, a_log_f, a_log_b, dt_bias_f, dt_bias_b, L=L)
    dt4 = jnp.transpose(dt.reshape(Lp, 2, G, R), (1, 2, 0, 3))
    acs4 = jnp.transpose(acs.reshape(Lp, 2, G, R), (1, 2, 0, 3))
    scol = jnp.concatenate([dt4, acs4], axis=-1)
    srow = jnp.transpose(acs4, (0, 1, 3, 2))
    y_f = _ssd_chunk(xbc, scol, srow, direction=0, d_inner=d_inner)
    y_b = _ssd_chunk(xbc, scol, srow, direction=1, d_inner=d_inner)
    d_skip_row = jnp.repeat(d_skip, SSD_HEAD_DIM).reshape(1, d_inner)
    a = _ssd_norm(y_f, y_b, xbc, z, d_skip_row, norm_w)
    return _out_ln(a, w_out, h, ln_g, ln_b)


def _fft_plan(L):
    N2 = _pick(L, (72, 64, 48, 32, 24, 16, 8))
    T1 = L // N2
    N1 = 2 * T1
    n = N1 * N2
    K1 = N1 // 2 + 1
    k1 = np.arange(K1, dtype=np.float64)[:, None]

    def stage1(T):
        th = 2.0 * np.pi * k1 * np.arange(T, dtype=np.float64)[None, :] / N1
        return np.cos(th), -np.sin(th)

    t2 = np.arange(N2, dtype=np.float64)[None, :]
    k2 = np.arange(N2, dtype=np.float64)[:, None]
    g = np.exp(-2j * np.pi * (t2 * k2 / N2)[None] - 2j * np.pi * (t2[None] * k1[:, :, None]) / n)
    gi = np.conj(np.transpose(g, (0, 2, 1)))

    def stack(m):
        return np.concatenate([np.concatenate([m.real, -m.imag], axis=2),
                               np.concatenate([m.imag, m.real], axis=2)], axis=1)

    t1 = np.arange(T1, dtype=np.float64)[:, None]
    kk = np.arange(K1, dtype=np.float64)[None, :]
    cw = np.where((kk == 0) | (kk == N1 // 2), 1.0, 2.0)
    th = 2.0 * np.pi * t1 * kk / N1
    f32 = lambda a: np.asarray(a, np.float32)
    fd_r, fd_i = stage1(T1)
    ff_r, ff_i = stage1(N1)
    return dict(N2=N2, T1=T1, N1=N1, K1=K1,
                fd_r=f32(fd_r), fd_i=f32(fd_i), ff_r=f32(ff_r), ff_i=f32(ff_i),
                g=f32(stack(g)), gi=f32(stack(gi)),
                fin_r=f32(cw * np.cos(th) / n), fin_i=f32(-cw * np.sin(th) / n))


def _dft1_body(fr_ref, fi_ref, x_ref, yr_ref, yi_ref):
    x = x_ref[...]
    yr_ref[...] = jnp.dot(fr_ref[...], x, preferred_element_type=F32)
    yi_ref[...] = jnp.dot(fi_ref[...], x, preferred_element_type=F32)


def _dft1(fr, fi, x2):
    K1, T = fr.shape
    N = x2.shape[1]
    tn = _pick(N, (4096, 2048, 1024, 512, 256, 128))
    mat = pl.BlockSpec((K1, T), lambda j: (0, 0))
    out = pl.BlockSpec((K1, tn), lambda j: (0, j))
    return pl.pallas_call(
        _dft1_body,
        grid=(N // tn,),
        in_specs=[mat, mat, pl.BlockSpec((T, tn), lambda j: (0, j))],
        out_specs=[out, out],
        out_shape=[jax.ShapeDtypeStruct((K1, N), F32)] * 2,
        compiler_params=_cparams("parallel"),
        name="dft_stage1",
    )(fr, fi, x2)


def _dft2_body(g_ref, yr_ref, yi_ref, z_ref):
    y = jnp.concatenate([yr_ref[...], yi_ref[...]], axis=0)
    z_ref[...] = jnp.dot(g_ref[...], y, preferred_element_type=F32)


def _dft2(g, yr, yi, N2):
    K1 = g.shape[0]
    C = yr.shape[1]
    tc = _pick(C, (2048, 1024, 512, 256, 128))
    yspec = pl.BlockSpec((N2, tc), lambda a, j: (a, j))
    return pl.pallas_call(
        _dft2_body,
        grid=(K1, C // tc),
        in_specs=[pl.BlockSpec((None, 2 * N2, 2 * N2), lambda a, j: (a, 0, 0)), yspec, yspec],
        out_specs=pl.BlockSpec((2 * N2, tc), lambda a, j: (a, j)),
        out_shape=jax.ShapeDtypeStruct((K1 * 2 * N2, C), F32),
        compiler_params=_cparams("parallel", "parallel"),
        name="dft_stage2",
    )(g, yr, yi)


def _spec_mul_body(g_ref, gi_ref, yr_ref, yi_ref, h_ref, qr_ref, qi_ref, *, N2):
    y = jnp.concatenate([yr_ref[...], yi_ref[...]], axis=0)
    z = jnp.dot(g_ref[...], y, preferred_element_type=F32)
    zr, zi = z[:N2], z[N2:]
    hr, hi = h_ref[0:N2, :], h_ref[N2:2 * N2, :]
    p = jnp.concatenate([zr * hr - zi * hi, zr * hi + zi * hr], axis=0)
    q = jnp.dot(gi_ref[...], p, preferred_element_type=F32)
    qr_ref[...] = q[:N2]
    qi_ref[...] = q[N2:]


def _spec_mul(g, gi, yr, yi, hspec, order, N2):
    K1 = g.shape[0]
    C = yr.shape[1]
    mat = pl.BlockSpec((None, 2 * N2, 2 * N2), lambda a: (a, 0, 0))
    yspec = pl.BlockSpec((N2, C), lambda a: (a, 0))
    return pl.pallas_call(
        functools.partial(_spec_mul_body, N2=N2),
        grid=(K1,),
        in_specs=[mat, mat, yspec, yspec, pl.BlockSpec((2 * N2, C), lambda a: (a, order))],
        out_specs=[yspec, yspec],
        out_shape=[jax.ShapeDtypeStruct((K1 * N2, C), F32)] * 2,
        compiler_params=_cparams("parallel"),
        name="spec_mul",
    )(g, gi, yr, yi, hspec)


def _idft_gate_body(fr_ref, fi_ref, qr_ref, qi_ref, z_ref, gate_ref, skip_ref, o_ref):
    y = jnp.dot(fr_ref[...], qr_ref[...], preferred_element_type=F32)
    y = y + jnp.dot(fi_ref[...], qi_ref[...], preferred_element_type=F32)
    o_ref[...] = gate_ref[...] * (y + z_ref[...] * skip_ref[...])


def _idft_gate(fr, fi, qr2, qi2, z2, gate2, skip_row):
    T1, K1 = fr.shape
    N = qr2.shape[1]
    tn = _pick(N, (4096, 2048, 1024, 512, 256, 128))
    mat = pl.BlockSpec((T1, K1), lambda j: (0, 0))
    qs = pl.BlockSpec((K1, tn), lambda j: (0, j))
    ts = pl.BlockSpec((T1, tn), lambda j: (0, j))
    return pl.pallas_call(
        _idft_gate_body,
        grid=(N // tn,),
        in_specs=[mat, mat, qs, qs, ts, ts, pl.BlockSpec((1, tn), lambda j: (0, j))],
        out_specs=ts,
        out_shape=jax.ShapeDtypeStruct((T1, N), F32),
        compiler_params=_cparams("parallel"),
        name="idft_gate",
    )(fr, fi, qr2, qi2, z2, gate2, skip_row)


def _hy_filter_body(feat_ref, w1_ref, b1_ref, f1_ref, w2_ref, b2_ref, f2_ref, w3_ref, b3_ref, dl_ref, o_ref,
                    *, L, tr):
    i = pl.program_id(0)
    feats = feat_ref[...]
    z = jnp.sin(f1_ref[...] * (jnp.dot(feats, w1_ref[...], preferred_element_type=F32) + b1_ref[...]))
    z = jnp.sin(f2_ref[...] * (jnp.dot(z, w2_ref[...], preferred_element_type=F32) + b2_ref[...]))
    r = jnp.dot(z, w3_ref[...], preferred_element_type=F32) + b3_ref[...]
    r = r * jnp.exp(-feats[:, 0:1] * dl_ref[...])
    o_ref[...] = jnp.where(_row_ids(i, tr) == L, 0.0, r)


def _hy_filter(L, fw1, fb1, freq1, fw2, fb2, freq2, fw3, fb3, width):
    pos = np.arange(2 * L)
    pos = np.where(pos < L, pos, (2 * L - pos) % L).astype(np.float64)
    tt = pos / (L - 1)
    w = 2.0 * np.pi * pos / L
    f = np.linspace(1e-4, HY_BANDS - 1, HY_BANDS)
    feats = np.concatenate([tt[:, None], np.cos(f[None] * w[:, None]), -np.sin(f[None] * w[:, None])], axis=1)
    feats = jnp.asarray(feats, F32)
    emb = feats.shape[1]
    hid = fw1.shape[1]
    n_ord = fw3.shape[1] // (2 * width)
    oc = n_ord * width
    deltas = np.abs(np.linspace(math.log(HY_TARGET) / HY_DECAY_PCT_SHORT,
                                math.log(HY_TARGET) / HY_DECAY_PCT_LONG, width))
    dl = jnp.asarray(np.tile(deltas, n_ord)[None], F32)
    tr = max(d for d in range(8, min(L, 512) + 1, 8) if L % d == 0)
    nt = L // tr
    row = lambda a: a.reshape(1, -1)
    full = lambda s: pl.BlockSpec(s, lambda i: (0, 0))
    return pl.pallas_call(
        functools.partial(_hy_filter_body, L=L, tr=tr),
        grid=(2 * nt,),
        in_specs=[pl.BlockSpec((tr, emb), lambda i: (i, 0)),
                  full((emb, hid)), full((1, hid)), full((1, hid)),
                  full((hid, hid)), full((1, hid)), full((1, hid)),
                  pl.BlockSpec((hid, oc), lambda i: (0, i // nt)),
                  pl.BlockSpec((1, oc), lambda i: (0, i // nt)),
                  full((1, oc))],
        out_specs=pl.BlockSpec((tr, oc), lambda i: (i, 0)),
        out_shape=jax.ShapeDtypeStruct((2 * L, oc), F32),
        compiler_params=_cparams("parallel"),
        name="hy_filter",
    )(feats, fw1, row(fb1), row(freq1), fw2, row(fb2), row(freq2), fw3, row(fb3), dl)


def _hyena_mixer(hb, h, L, w_in, conv_w, conv_b, fw1, fb1, freq1, fw2, fb2, freq2, fw3, fb3, filt_skip, w_out,
                 ln_g, ln_b):
    Lp = hb.shape[0]
    W = w_out.shape[0]
    n_ord = filt_skip.shape[0]
    u_pre = _proj(hb, w_in, 0, 3 * W)
    u = _conv(u_pre, conv_w, conv_b.reshape(1, 3 * W), L=L, col0=0, ncols=3 * W, silu=False, l2norm=False)
    plan = _fft_plan(L)
    N2, T1, N1, K1 = plan["N2"], plan["T1"], plan["N1"], plan["K1"]
    cst = {k: jnp.asarray(v) for k, v in plan.items() if isinstance(v, np.ndarray)}
    resp = _hy_filter(L, fw1, fb1, freq1, fw2, fb2, freq2, fw3, fb3, W)
    hr, hi = _dft1(cst["ff_r"], cst["ff_i"], resp.reshape(N1, N2 * n_ord * W))
    hspec = _dft2(cst["g"], hr.reshape(K1 * N2, n_ord * W), hi.reshape(K1 * N2, n_ord * W), N2)
    to_fft = lambda a: a[:L].reshape(T1, N2 * W)
    gates = [to_fft(u[:, o * W:(o + 1) * W]) for o in range(n_ord)]
    z = to_fft(u[:, n_ord * W:(n_ord + 1) * W])
    for o in range(n_ord):
        yr, yi = _dft1(cst["fd_r"], cst["fd_i"], z)
        qr, qi = _spec_mul(cst["g"], cst["gi"], yr.reshape(K1 * N2, W), yi.reshape(K1 * N2, W), hspec, o, N2)
        skip_row = jnp.tile(filt_skip[o], N2).reshape(1, N2 * W)
        z = _idft_gate(cst["fin_r"], cst["fin_i"], qr.reshape(K1, N2 * W), qi.reshape(K1, N2 * W), z, gates[o],
                       skip_row)
    a = jnp.pad(z.reshape(L, W), ((0, Lp - L), (0, 0))).astype(BF16)
    return _out_ln(a, w_out, h, ln_g, ln_b)


def _router_body(h_ref, w_ref, b_ref, o_ref):
    logits = jnp.dot(h_ref[...], w_ref[...], precision=HIGHEST, preferred_element_type=F32) + b_ref[...]
    tm = logits.shape[0]
    lane = lax.broadcasted_iota(jnp.int32, (tm, LANES), 1)
    neg = -jnp.inf
    G, E, PG = MOE_GROUPS, MOE_EXPERTS, MOE_PER_GROUP
    glog = jnp.where(lane < G, logits, neg)
    gmax = jnp.max(glog, axis=-1, keepdims=True)
    gsel = jnp.min(jnp.where(glog == gmax, lane, LANES), axis=-1, keepdims=True)
    gw = 1.0 / jnp.sum(jnp.exp(glog - gmax), axis=-1, keepdims=True)
    lo = G + gsel * PG
    elog = jnp.where(jnp.logical_and(lane >= lo, lane < lo + PG), logits, neg)
    m1 = jnp.max(elog, axis=-1, keepdims=True)
    i1 = jnp.min(jnp.where(elog == m1, lane, LANES), axis=-1, keepdims=True)
    elog2 = jnp.where(lane == i1, neg, elog)
    m2 = jnp.max(elog2, axis=-1, keepdims=True)
    i2 = jnp.min(jnp.where(elog2 == m2, lane, LANES), axis=-1, keepdims=True)
    e2 = jnp.exp(m2 - m1)
    w1 = gw / (1.0 + e2)
    w2 = gw * e2 / (1.0 + e2)
    out = jnp.where(lane == 0, (i1 - G).astype(F32),
                    jnp.where(lane == 1, (i2 - G).astype(F32),
                              jnp.where(lane == 2, w1, jnp.where(lane == 3, w2, 0.0))))
    o_ref[...] = out


def _router(h, wr_group, br_group, wr_expert, br_expert):
    Lp, D = h.shape
    used = MOE_GROUPS + MOE_EXPERTS
    w = jnp.pad(jnp.concatenate([wr_group, wr_expert], axis=1), ((0, 0), (0, LANES - used)))
    b = jnp.pad(jnp.concatenate([br_group, br_expert]), (0, LANES - used)).reshape(1, LANES)
    tm = _pick(Lp, (768, 512, 256))
    return pl.pallas_call(
        _router_body,
        grid=(Lp // tm,),
        in_specs=[pl.BlockSpec((tm, D), lambda i: (i, 0)),
                  pl.BlockSpec((D, LANES), lambda i: (0, 0)),
                  pl.BlockSpec((1, LANES), lambda i: (0, 0))],
        out_specs=pl.BlockSpec((tm, LANES), lambda i: (i, 0)),
        out_shape=jax.ShapeDtypeStruct((Lp, LANES), F32),
        compiler_params=_cparams("parallel"),
        name="router",
    )(h, w, b)


def _row_copy(src_hbm, dst_vmem, sem, src_row, dst_row):
    return pltpu.make_async_copy(src_hbm.at[pl.ds(src_row, 1)], dst_vmem.at[pl.ds(dst_row, 1)], sem)


def _gather_body(tok_ref, h_hbm, o_ref, sem, *, rows):
    base = pl.program_id(0) * rows

    def issue(s, carry):
        _row_copy(h_hbm, o_ref, sem, tok_ref[base + s], s).start()
        return carry

    lax.fori_loop(0, rows, issue, 0)

    def drain(s, carry):
        _row_copy(h_hbm, o_ref, sem, 0, s).wait()
        return carry

    lax.fori_loop(0, rows, drain, 0)


def _gather_rows(src, idx, rows_per_step):
    P = idx.shape[0]
    D = src.shape[1]
    return pl.pallas_call(
        functools.partial(_gather_body, rows=rows_per_step),
        grid_spec=pltpu.PrefetchScalarGridSpec(
            num_scalar_prefetch=1,
            grid=(P // rows_per_step,),
            in_specs=[pl.BlockSpec(memory_space=pl.ANY)],
            out_specs=pl.BlockSpec((rows_per_step, D), lambda b, tok: (b, 0)),
            scratch_shapes=[pltpu.SemaphoreType.DMA(())]),
        out_shape=jax.ShapeDtypeStruct((P, D), src.dtype),
        compiler_params=_cparams("arbitrary"),
        name="moe_gather",
    )(idx, src)


def _expert_body(be_ref, nu_ref, x_ref, wg_ref, wu_ref, wd_ref, o_ref):
    b = pl.program_id(0)

    @pl.when(b < nu_ref[0])
    def _():
        x = x_ref[...].astype(BF16)
        g = jnp.dot(x, wg_ref[...], preferred_element_type=F32)
        u = jnp.dot(x, wu_ref[...], preferred_element_type=F32)
        hid = (g * _sigmoid(g)) * u
        o_ref[...] = jnp.dot(hid.astype(BF16), wd_ref[...], preferred_element_type=F32)

    @pl.when(b >= nu_ref[0])
    def _():
        o_ref[...] = jnp.zeros_like(o_ref)


def _experts(xs, blk_expert, n_used, wg, wu, wd):
    P, D = xs.shape
    FF = wg.shape[2]
    nb = P // MOE_BLOCK
    return pl.pallas_call(
        _expert_body,
        grid_spec=pltpu.PrefetchScalarGridSpec(
            num_scalar_prefetch=2,
            grid=(nb,),
            in_specs=[pl.BlockSpec((MOE_BLOCK, D), lambda b, be, nu: (b, 0)),
                      pl.BlockSpec((None, D, FF), lambda b, be, nu: (be[b], 0, 0)),
                      pl.BlockSpec((None, D, FF), lambda b, be, nu: (be[b], 0, 0)),
                      pl.BlockSpec((None, FF, D), lambda b, be, nu: (be[b], 0, 0))],
            out_specs=pl.BlockSpec((MOE_BLOCK, D), lambda b, be, nu: (b, 0))),
        out_shape=jax.ShapeDtypeStruct((P, D), F32),
        compiler_params=_cparams("arbitrary"),
        name="moe_experts",
    )(blk_expert, n_used, xs, wg, wu, wd)


def _combine_body(dest_ref, yb_hbm, rw_ref, h_ref, g_ref, b_ref, o_ref, ob_ref, buf_ref, sem, *, L, tm):
    i = pl.program_id(0)
    base = i * tm

    def issue(s, carry):
        _row_copy(yb_hbm, buf_ref.at[0], sem, dest_ref[2 * (base + s)], s).start()
        _row_copy(yb_hbm, buf_ref.at[1], sem, dest_ref[2 * (base + s) + 1], s).start()
        return carry

    lax.fori_loop(0, tm, issue, 0)

    def drain(s, carry):
        _row_copy(yb_hbm, buf_ref.at[0], sem, 0, s).wait()
        _row_copy(yb_hbm, buf_ref.at[1], sem, 0, s).wait()
        return carry

    lax.fori_loop(0, tm, drain, 0)
    rw = rw_ref[...]
    f = buf_ref[0] * rw[:, 2:3] + buf_ref[1] * rw[:, 3:4]
    f = jnp.where(_row_ids(i, tm) < L, f, 0.0)
    y = _layer_norm_rows(DN_ALPHA * h_ref[...] + f, g_ref[...], b_ref[...])
    o_ref[...] = y
    ob_ref[...] = y.astype(BF16)


def _combine_ln(yb, dest, rw, h, g, b, *, L):
    Lp, D = h.shape
    tm = _pick(Lp, (128,))
    row = pl.BlockSpec((tm, D), lambda i, d: (i, 0))
    par = pl.BlockSpec((1, D), lambda i, d: (0, 0))
    return pl.pallas_call(
        functools.partial(_combine_body, L=L, tm=tm),
        grid_spec=pltpu.PrefetchScalarGridSpec(
            num_scalar_prefetch=1,
            grid=(Lp // tm,),
            in_specs=[pl.BlockSpec(memory_space=pl.ANY),
                      pl.BlockSpec((tm, LANES), lambda i, d: (i, 0)),
                      row, par, par],
            out_specs=[row, row],
            scratch_shapes=[pltpu.VMEM((2, tm, D), F32), pltpu.SemaphoreType.DMA(())]),
        out_shape=[jax.ShapeDtypeStruct((Lp, D), F32), jax.ShapeDtypeStruct((Lp, D), BF16)],
        compiler_params=_cparams("arbitrary"),
        name="moe_combine",
    )(dest, yb, rw, h, g.reshape(1, D), b.reshape(1, D))


def _moe(h, L, wr_group, br_group, wr_expert, br_expert, w_gate, w_up, w_down, ln_g, ln_b):
    Lp, D = h.shape
    E, BLK = MOE_EXPERTS, MOE_BLOCK
    rw = _router(h, wr_group, br_group, wr_expert, br_expert)
    flat_e = rw[:L, 0:2].astype(jnp.int32).reshape(-1)
    A = 2 * L
    onehot = (flat_e[:, None] == jnp.arange(E, dtype=jnp.int32)[None, :]).astype(jnp.int32)
    csum = jnp.cumsum(onehot, axis=0)
    rank = jnp.take_along_axis(csum, flat_e[:, None], axis=1)[:, 0] - 1
    counts = csum[-1]
    padded = (counts + BLK - 1) // BLK * BLK
    pend = jnp.cumsum(padded)
    pstart = pend - padded
    dest = (pstart[flat_e] + rank).astype(jnp.int32)
    nb = -(-A // BLK) + E
    tok = jnp.arange(A, dtype=jnp.int32) // 2
    slot_tok = jnp.full((nb * BLK,), L, jnp.int32).at[dest].set(tok)
    blk_expert = jnp.minimum(jnp.searchsorted(pend, jnp.arange(nb, dtype=jnp.int32) * BLK, side="right"),
                             E - 1).astype(jnp.int32)
    n_used = (pend[-1:] // BLK).astype(jnp.int32)
    dest_pad = jnp.pad(dest, (0, 2 * (Lp - L)))
    xs = _gather_rows(h, slot_tok, BLK)
    yb = _experts(xs, blk_expert, n_used, w_gate.astype(BF16), w_up.astype(BF16), w_down.astype(BF16))
    return _combine_ln(yb, dest_pad, rw, h, ln_g, ln_b, L=L)


def kernel(x, meta_tokens, ln_mix_g, ln_mix_b, ln_ffn_g, ln_ffn_b, gdn_w_in, gdn_conv_w, gdn_a_log_f, gdn_a_log_b, gdn_dt_bias_f, gdn_dt_bias_b, gdn_norm_w, gdn_w_out, ssd_w_in, ssd_conv_w, ssd_conv_b, ssd_a_log_f, ssd_a_log_b, ssd_dt_bias_f, ssd_dt_bias_b, ssd_d_skip, ssd_norm_w, ssd_w_out, hy_w_in, hy_conv_w, hy_conv_b, hy_fw1, hy_fb1, hy_freq1, hy_fw2, hy_fb2, hy_freq2, hy_fw3, hy_fb3, hy_filt_skip, hy_w_out, moe_wr_group, moe_br_group, moe_wr_expert, moe_br_expert, moe_w_gate, moe_w_up, moe_w_down):
    assert x.shape[0] == 1
    n_meta = meta_tokens.shape[0]
    L = n_meta + x.shape[1]
    Lp = -(-L // 256) * 256
    depth = ln_mix_g.shape[0]
    h = jnp.concatenate([meta_tokens.astype(x.dtype), x[0]], axis=0)
    h = jnp.pad(h, ((0, Lp - L), (0, 0)))
    hb = h.astype(BF16)
    for i in range(depth):
        kind, j = i % N_MIXERS, i // N_MIXERS
        if kind == 0:
            h, hb = _gdn_mixer(hb, h, L, gdn_w_in[j], gdn_conv_w[j], gdn_a_log_f[j], gdn_a_log_b[j],
                               gdn_dt_bias_f[j], gdn_dt_bias_b[j], gdn_norm_w[j], gdn_w_out[j],
                               ln_mix_g[i], ln_mix_b[i])
        elif kind == 1:
            h, hb = _ssd_mixer(hb, h, L, ssd_w_in[j], ssd_conv_w[j], ssd_conv_b[j], ssd_a_log_f[j],
                               ssd_a_log_b[j], ssd_dt_bias_f[j], ssd_dt_bias_b[j], ssd_d_skip[j],
                               ssd_norm_w[j], ssd_w_out[j], ln_mix_g[i], ln_mix_b[i])
        else:
            h, hb = _hyena_mixer(hb, h, L, hy_w_in[j], hy_conv_w[j], hy_conv_b[j], hy_fw1[j], hy_fb1[j],
                                 hy_freq1[j], hy_fw2[j], hy_fb2[j], hy_freq2[j], hy_fw3[j], hy_fb3[j],
                                 hy_filt_skip[j], hy_w_out[j], ln_mix_g[i], ln_mix_b[i])
        h, hb = _moe(h, L, moe_wr_group[i], moe_br_group[i], moe_wr_expert[i], moe_br_expert[i],
                     moe_w_gate[i], moe_w_up[i], moe_w_down[i], ln_ffn_g[i], ln_ffn_b[i])
    return h[n_meta:L][None]
```

```python
import functools
import math

import numpy as np
import jax
import jax.numpy as jnp
from jax import lax
from jax.experimental import pallas as pl
from jax.experimental.pallas import tpu as pltpu

F32 = jnp.float32
BF16 = jnp.bfloat16
HIGHEST = lax.Precision.HIGHEST

LANES = 128
VMEM_LIMIT = 56 * 1024 * 1024

DEPTH = 4
N_MIXERS = 3
DN_ALPHA = (2.0 * DEPTH) ** 0.25
LN_EPS = 1e-5
RMS_EPS = 1e-6

GDN_QK_HEADS = 16
GDN_V_HEADS = 32
GDN_HEAD_DIM = 128
GDN_QK_DIM = GDN_QK_HEADS * GDN_HEAD_DIM
GDN_V_DIM = GDN_V_HEADS * GDN_HEAD_DIM
GDN_CONV_CH = 2 * GDN_QK_DIM + GDN_V_DIM
GDN_CHUNK = 64
GDN_ROWS_PER_STEP = 256
GDN_QK_HEADS_PER_STEP = 2

SSD_HEAD_DIM = 64
SSD_GROUPS = 8
SSD_STATE = 128
SSD_CHUNK = 128
SSD_ROWS_PER_STEP = 256

HY_BANDS = 16
HY_TARGET = 1e-2
HY_DECAY_PCT_SHORT = 0.3
HY_DECAY_PCT_LONG = 1.5

MOE_GROUPS = 4
MOE_PER_GROUP = 8
MOE_EXPERTS = MOE_GROUPS * MOE_PER_GROUP
MOE_BLOCK = 128


def _cparams(*sem):
    return pltpu.CompilerParams(dimension_semantics=sem, vmem_limit_bytes=VMEM_LIMIT)


def _pick(n, candidates):
    for c in candidates:
        if n % c == 0:
            return c
    raise ValueError(f"no tile in {candidates} divides {n}")


def _row_ids(i, tr):
    return i * tr + lax.broadcasted_iota(jnp.int32, (tr, 1), 0)


def _layer_norm_rows(y, g, b):
    mu = jnp.mean(y, axis=-1, keepdims=True)
    d = y - mu
    var = jnp.mean(d * d, axis=-1, keepdims=True)
    return d * lax.rsqrt(var + LN_EPS) * g + b


def _softplus(x):
    return jnp.maximum(x, 0.0) + jnp.log(1.0 + jnp.exp(-jnp.abs(x)))


def _sigmoid(x):
    return 1.0 / (1.0 + jnp.exp(-x))


def _proj_body(a_ref, w_ref, o_ref):
    o_ref[...] = jnp.dot(a_ref[...], w_ref[...].astype(BF16), preferred_element_type=F32)


def _proj(a, w, col0, ncols):
    Lp, K = a.shape
    tn = _pick(ncols, (1024, 512, 256, 128))
    assert col0 % tn == 0
    tm = _pick(Lp, (1408, 1056, 768, 512, 256))
    return pl.pallas_call(
        _proj_body,
        grid=(ncols // tn, Lp // tm),
        in_specs=[pl.BlockSpec((tm, K), lambda j, i: (i, 0)),
                  pl.BlockSpec((K, tn), lambda j, i: (0, col0 // tn + j))],
        out_specs=pl.BlockSpec((tm, tn), lambda j, i: (i, j)),
        out_shape=jax.ShapeDtypeStruct((Lp, ncols), F32),
        compiler_params=_cparams("parallel", "parallel"),
        name="proj",
    )(a, w)


def _out_ln_body(a_ref, w_ref, h_ref, g_ref, b_ref, o_ref, ob_ref, acc_ref, *, nk, rows):
    i = pl.program_id(0)
    k = pl.program_id(1)

    @pl.when(k == 0)
    def _():
        acc_ref[...] = jnp.zeros_like(acc_ref)

    a = a_ref[...]
    if rows is not None:
        a = jnp.where(_row_ids(i, a.shape[0]) < rows, a, 0.0)
    acc_ref[...] += jnp.dot(a.astype(BF16), w_ref[...].astype(BF16), preferred_element_type=F32)

    @pl.when(k == nk - 1)
    def _():
        y = _layer_norm_rows(DN_ALPHA * h_ref[...] + acc_ref[...], g_ref[...], b_ref[...])
        o_ref[...] = y
        ob_ref[...] = y.astype(BF16)


def _out_ln(a, w, h, g, b, rows=None):
    Lp = h.shape[0]
    K = a.shape[1]
    assert a.shape[0] == (Lp if rows is None else rows)
    D = w.shape[1]
    tm = _pick(Lp, (704, 512, 256))
    tk = _pick(K, (512, 256))
    nk = K // tk
    return pl.pallas_call(
        functools.partial(_out_ln_body, nk=nk, rows=rows),
        grid=(Lp // tm, nk),
        in_specs=[pl.BlockSpec((tm, tk), lambda i, k: (i, k)),
                  pl.BlockSpec((tk, D), lambda i, k: (k, 0)),
                  pl.BlockSpec((tm, D), lambda i, k: (i, 0)),
                  pl.BlockSpec((1, D), lambda i, k: (0, 0)),
                  pl.BlockSpec((1, D), lambda i, k: (0, 0))],
        out_specs=[pl.BlockSpec((tm, D), lambda i, k: (i, 0)),
                   pl.BlockSpec((tm, D), lambda i, k: (i, 0))],
        out_shape=[jax.ShapeDtypeStruct((Lp, D), F32), jax.ShapeDtypeStruct((Lp, D), BF16)],
        scratch_shapes=[pltpu.VMEM((tm, D), F32)],
        compiler_params=_cparams("parallel", "arbitrary"),
        name="out_ln",
    )(a, w, h, g.reshape(1, D), b.reshape(1, D))


def _conv_body(prev_ref, cur_ref, next_ref, w_ref, b_ref, o_ref, *, taps, L, tr, nblk, silu, l2norm):
    i = pl.program_id(0)
    half = taps // 2
    r8 = lax.broadcasted_iota(jnp.int32, (8, 1), 0)
    cur = jnp.where(_row_ids(i, tr) < L, cur_ref[...], 0.0)
    prev = jnp.where(jnp.logical_and(i > 0, i * tr - 8 + r8 < L), prev_ref[...], 0.0)
    nxt = jnp.where(jnp.logical_and(i < nblk - 1, (i + 1) * tr + r8 < L), next_ref[...], 0.0)
    ext = jnp.concatenate([prev, cur, nxt], axis=0)
    acc = ext[8 - half:8 - half + tr] * w_ref[0:1, :]
    for k in range(1, taps):
        s = 8 + k - half
        acc = acc + ext[s:s + tr] * w_ref[k:k + 1, :]
    acc = acc + b_ref[...]
    if silu:
        acc = acc * _sigmoid(acc)
    acc = jnp.where(_row_ids(i, tr) < L, acc, 0.0)
    if l2norm:
        tc = acc.shape[1]
        for hh in range(tc // LANES):
            blk = acc[:, hh * LANES:(hh + 1) * LANES]
            ss = jnp.sum(blk * blk, axis=-1, keepdims=True)
            o_ref[:, hh * LANES:(hh + 1) * LANES] = blk * lax.rsqrt(ss + 1e-6)
    else:
        o_ref[...] = acc


def _conv(pre, w, b, *, L, col0, ncols, silu, l2norm, out_rows=None):
    Lp = pre.shape[0]
    out_rows = Lp if out_rows is None else out_rows
    taps = w.shape[0]
    tr = max(d for d in range(8, 769, 8) if out_rows % d == 0)
    tc = _pick(ncols, (512, 256, 128))
    assert col0 % tc == 0
    nblk = out_rows // tr
    c0 = col0 // tc
    r8 = tr // 8
    last8 = Lp // 8 - 1
    return pl.pallas_call(
        functools.partial(_conv_body, taps=taps, L=L, tr=tr, nblk=nblk, silu=silu, l2norm=l2norm),
        grid=(nblk, ncols // tc),
        in_specs=[pl.BlockSpec((8, tc), lambda i, j: (jnp.maximum(i * r8 - 1, 0), c0 + j)),
                  pl.BlockSpec((tr, tc), lambda i, j: (i, c0 + j)),
                  pl.BlockSpec((8, tc), lambda i, j: (jnp.minimum((i + 1) * r8, last8), c0 + j)),
                  pl.BlockSpec((taps, tc), lambda i, j: (0, c0 + j)),
                  pl.BlockSpec((1, tc), lambda i, j: (0, c0 + j))],
        out_specs=pl.BlockSpec((tr, tc), lambda i, j: (i, j)),
        out_shape=jax.ShapeDtypeStruct((out_rows, ncols), F32),
        compiler_params=_cparams("parallel", "parallel"),
        name="conv",
    )(pre, pre, pre, w, b)


def _tri(n, upper):
    r = lax.broadcasted_iota(jnp.int32, (n, n), 0)
    c = lax.broadcasted_iota(jnp.int32, (n, n), 1)
    return ((r <= c) if upper else (r >= c)).astype(F32)


def _gdn_gates_body(raw_ref, alog_ref, dtb_ref, o_ref, *, L, tr):
    i = pl.program_id(0)
    nh = GDN_V_HEADS
    lane = lax.broadcasted_iota(jnp.int32, (1, 4 * nh), 1)
    is_g = (lane // nh) % 2 == 0
    raw = raw_ref[...]
    g = -jnp.exp(alog_ref[...]) * _softplus(raw + dtb_ref[...])
    val = jnp.where(is_g, g, _sigmoid(raw))
    val = jnp.where(_row_ids(i, tr) < L, val, 0.0)
    C = GDN_CHUNK
    lo, up = _tri(C, False), _tri(C, True)
    for c in range(tr // C):
        blk = val[c * C:(c + 1) * C]
        pre = jnp.dot(lo, blk, precision=HIGHEST, preferred_element_type=F32)
        suf = jnp.dot(up, blk, precision=HIGHEST, preferred_element_type=F32)
        o_ref[c * C:(c + 1) * C, :] = jnp.where(lane < nh, pre, jnp.where(lane // nh == 2, suf, blk))


def _gdn_gates(raw, a_log_f, a_log_b, dt_bias_f, dt_bias_b, *, L):
    Lp, W = raw.shape
    zeros = jnp.zeros_like(a_log_f)
    alog = jnp.concatenate([a_log_f, zeros, a_log_b, zeros]).reshape(1, W)
    dtb = jnp.concatenate([dt_bias_f, zeros, dt_bias_b, zeros]).reshape(1, W)
    tr = _pick(Lp, (256,))
    return pl.pallas_call(
        functools.partial(_gdn_gates_body, L=L, tr=tr),
        grid=(Lp // tr,),
        in_specs=[pl.BlockSpec((tr, W), lambda i: (i, 0)),
                  pl.BlockSpec((1, W), lambda i: (0, 0)),
                  pl.BlockSpec((1, W), lambda i: (0, 0))],
        out_specs=pl.BlockSpec((tr, W), lambda i: (i, 0)),
        out_shape=jax.ShapeDtypeStruct((Lp, W), F32),
        compiler_params=_cparams("parallel"),
        name="gdn_gates",
    )(raw, alog, dtb)


def _gdn_chunk_body(q_ref, k_ref, v_ref, gcol_ref, grow_ref, o_ref, s_ref, qm_ref, on_ref, *, reverse, nchunks,
                    nqk):
    i = pl.program_id(1)

    @pl.when(i == 0)
    def _():
        s_ref[...] = jnp.zeros_like(s_ref)

    C = GDN_CHUNK
    Dh = GDN_HEAD_DIM
    rr = lax.broadcasted_iota(jnp.int32, (C, C), 0)
    cc = lax.broadcasted_iota(jnp.int32, (C, C), 1)
    incl = (rr <= cc) if reverse else (rr >= cc)
    strict = (rr < cc) if reverse else (rr > cc)
    eye = (rr == cc).astype(F32)
    scale = Dh ** -0.5
    last = 0 if reverse else C - 1
    chunks = range(nchunks)
    heads = range(2 * nqk)
    chains = [(c, hv) for c in chunks for hv in heads]
    sls = [slice(c * C, (c + 1) * C) for c in chunks]
    hsl = [slice(hv * Dh, (hv + 1) * Dh) for hv in heads]
    dot = functools.partial(jnp.dot, preferred_element_type=F32)

    kc = {(c, hq): k_ref[sls[c], hsl[hq]] for c in chunks for hq in range(nqk)}
    qc = {(c, hq): q_ref[sls[c], hsl[hq]] * scale for c in chunks for hq in range(nqk)}
    g_col = {(c, hv): gcol_ref[hv // 2, sls[c], hv % 2:hv % 2 + 1] for c, hv in chains}
    b_col = {(c, hv): gcol_ref[hv // 2, sls[c], 2 + hv % 2:3 + hv % 2] for c, hv in chains}
    g_row = {(c, hv): grow_ref[hv // 2, hv % 2:hv % 2 + 1, sls[c]] for c, hv in chains}
    g_last = {ch: g_row[ch][:, last:last + 1] for ch in chains}
    kb = {(c, hv): kc[c, hv // 2] * b_col[c, hv] for c, hv in chains}
    kk = {(c, hq): lax.dot_general(jnp.concatenate([kb[c, 2 * hq], kb[c, 2 * hq + 1], qc[c, hq]], axis=0),
                                   kc[c, hq], (((1,), (1,)), ((), ())), preferred_element_type=F32)
          for c in chunks for hq in range(nqk)}
    decay = {ch: jnp.where(incl, jnp.exp(jnp.where(incl, g_col[ch] - g_row[ch], 0.0)), 0.0) for ch in chains}
    p = {(c, hv): -jnp.where(strict, kk[c, hv // 2][(hv % 2) * C:(hv % 2 + 1) * C] * decay[c, hv], 0.0)
         for c, hv in chains}
    intra = {(c, hv): jnp.where(incl, kk[c, hv // 2][2 * C:] * decay[c, hv], 0.0) for c, hv in chains}
    t = {ch: eye + p[ch] for ch in chains}
    pk = {ch: dot(p[ch], p[ch]) for ch in chains}
    steps = int(math.log2(C)) - 1
    for lvl in range(steps):
        if lvl < steps - 1:
            pp = {ch: dot(pk[ch], jnp.concatenate([pk[ch], t[ch]], axis=1)) for ch in chains}
            pk = {ch: pp[ch][:, :C] for ch in chains}
            t = {ch: t[ch] + pp[ch][:, C:] for ch in chains}
        else:
            t = {ch: t[ch] + dot(pk[ch], t[ch]) for ch in chains}
    eg = {ch: jnp.exp(g_col[ch]) for ch in chains}
    uw = {(c, hv): dot(t[c, hv], jnp.concatenate([v_ref[sls[c], hsl[hv]] * b_col[c, hv],
                                                  kb[c, hv] * eg[c, hv]], axis=1)) for c, hv in chains}
    iu = {ch: dot(intra[ch], uw[ch]) for ch in chains}
    ku = {(c, hv): lax.dot_general(kc[c, hv // 2] * jnp.exp(g_last[c, hv] - g_col[c, hv]), uw[c, hv],
                                   (((0,), (0,)), ((), ())), preferred_element_type=F32) for c, hv in chains}
    for n, (c, hv) in enumerate(chains):
        qm_ref[n, 0:C, :] = (qc[c, hv // 2] * eg[c, hv] - iu[c, hv][:, Dh:]).astype(BF16)
        qm_ref[n, C:, :] = (-ku[c, hv][:, Dh:]).astype(BF16)
        on_ref[n, 0:C, :] = iu[c, hv][:, :Dh]
        on_ref[n, C:, :] = ku[c, hv][:, :Dh]

    states = [s_ref[hv] for hv in heads]
    for c in (reversed(chunks) if reverse else chunks):
        for hv in heads:
            n = c * len(heads) + hv
            res = dot(qm_ref[n], states[hv].astype(BF16)) + on_ref[n]
            o_ref[sls[c], hsl[hv]] = res[:C]
            states[hv] = states[hv] * jnp.exp(g_last[c, hv]) + res[C:]
    for hv in heads:
        s_ref[hv] = states[hv]


def _gdn_chunk(qk, v, gcol, grow, *, direction):
    Lp = qk.shape[0]
    Dh = GDN_HEAD_DIM
    C = GDN_CHUNK
    tb = _pick(Lp, (GDN_ROWS_PER_STEP,))
    nqk = GDN_QK_HEADS_PER_STEP
    nblk = Lp // tb
    nchains = 2 * nqk * tb // C
    reverse = direction == 1

    def blk(i):
        return nblk - 1 - i if reverse else i

    return pl.pallas_call(
        functools.partial(_gdn_chunk_body, reverse=reverse, nchunks=tb // C, nqk=nqk),
        grid=(GDN_QK_HEADS // nqk, nblk),
        in_specs=[pl.BlockSpec((tb, nqk * Dh), lambda h, i: (blk(i), h)),
                  pl.BlockSpec((tb, nqk * Dh), lambda h, i: (blk(i), GDN_QK_HEADS // nqk + h)),
                  pl.BlockSpec((tb, 2 * nqk * Dh), lambda h, i: (blk(i), h)),
                  pl.BlockSpec((None, nqk, tb, 4), lambda h, i: (direction, h, blk(i), 0)),
                  pl.BlockSpec((None, nqk, 2, tb), lambda h, i: (direction, h, 0, blk(i)))],
        out_specs=pl.BlockSpec((tb, 2 * nqk * Dh), lambda h, i: (blk(i), h)),
        out_shape=jax.ShapeDtypeStruct((Lp, GDN_V_DIM), F32),
        scratch_shapes=[pltpu.VMEM((2 * nqk, Dh, Dh), F32),
                        pltpu.VMEM((nchains, C + Dh, Dh), BF16),
                        pltpu.VMEM((nchains, C + Dh, Dh), F32)],
        compiler_params=_cparams("parallel", "arbitrary"),
        name="gdn_chunk_bwd" if reverse else "gdn_chunk_fwd",
    )(qk, qk, v, gcol, grow)


def _gdn_norm_body(of_ref, ob_ref, z_ref, nw_ref, o_ref):
    tc = of_ref.shape[1]
    for hh in range(tc // LANES):
        sl = slice(hh * LANES, (hh + 1) * LANES)
        o = of_ref[:, sl] + ob_ref[:, sl]
        o = o * lax.rsqrt(jnp.mean(o * o, axis=-1, keepdims=True) + RMS_EPS)
        z = z_ref[:, sl]
        o_ref[:, sl] = (o * nw_ref[...] * (z * _sigmoid(z))).astype(BF16)


def _gdn_norm(o_f, o_b, z, norm_w):
    Lp, W = o_f.shape
    tr = _pick(Lp, (768, 512, 256))
    tc = 512
    spec = pl.BlockSpec((tr, tc), lambda i, j: (i, j))
    return pl.pallas_call(
        _gdn_norm_body,
        grid=(Lp // tr, W // tc),
        in_specs=[spec, spec, spec, pl.BlockSpec((1, LANES), lambda i, j: (0, 0))],
        out_specs=spec,
        out_shape=jax.ShapeDtypeStruct((Lp, W), BF16),
        compiler_params=_cparams("parallel", "parallel"),
        name="gdn_norm",
    )(o_f, o_b, z, norm_w.reshape(1, LANES))


def _gdn_mixer(hb, h, L, w_in, conv_w, a_log_f, a_log_b, dt_bias_f, dt_bias_b, norm_w, w_out, ln_g, ln_b):
    Lp = hb.shape[0]
    qkv_pre = _proj(hb, w_in, 0, GDN_CONV_CH)
    z = _proj(hb, w_in, GDN_CONV_CH, GDN_V_DIM)
    graw = _proj(hb, w_in, GDN_CONV_CH + GDN_V_DIM, 4 * GDN_V_HEADS)
    zero_b = jnp.zeros((1, GDN_CONV_CH), F32)
    qk = _conv(qkv_pre, conv_w, zero_b, L=L, col0=0, ncols=2 * GDN_QK_DIM, silu=True, l2norm=True)
    v = _conv(qkv_pre, conv_w, zero_b, L=L, col0=2 * GDN_QK_DIM, ncols=GDN_V_DIM, silu=True, l2norm=False)
    gpk = _gdn_gates(graw, a_log_f, a_log_b, dt_bias_f, dt_bias_b, L=L)
    g5 = gpk.reshape(Lp, 2, 2, GDN_QK_HEADS, 2)
    gcol = jnp.transpose(g5, (1, 3, 0, 2, 4)).reshape(2, GDN_QK_HEADS, Lp, 4)
    grow = jnp.transpose(g5[:, :, 0], (1, 2, 3, 0))
    o_f = _gdn_chunk(qk, v, gcol, grow, direction=0)
    o_b = _gdn_chunk(qk, v, gcol, grow, direction=1)
    a = _gdn_norm(o_f, o_b, z, norm_w)
    return _out_ln(a, w_out, h, ln_g, ln_b)


def _ssd_dt_body(raw_ref, alog_ref, dtb_ref, dt_ref, acs_ref, *, L, tr, nheads):
    i = pl.program_id(0)
    lane = lax.broadcasted_iota(jnp.int32, (1, 2 * nheads), 1)
    dt = _softplus(raw_ref[...] + dtb_ref[...])
    dt = jnp.where(_row_ids(i, tr) < L, dt, 0.0)
    dt_ref[...] = dt
    da = dt * (-jnp.exp(alog_ref[...]))
    C = SSD_CHUNK
    lo, up = _tri(C, False), _tri(C, True)
    for c in range(tr // C):
        blk = da[c * C:(c + 1) * C]
        pre = jnp.dot(lo, blk, precision=HIGHEST, preferred_element_type=F32)
        suf = jnp.dot(up, blk, precision=HIGHEST, preferred_element_type=F32)
        acs_ref[c * C:(c + 1) * C, :] = jnp.where(lane < nheads, pre, suf)


def _ssd_dt(raw, a_log_f, a_log_b, dt_bias_f, dt_bias_b, *, L):
    Lp, W = raw.shape
    alog = jnp.concatenate([a_log_f, a_log_b]).reshape(1, W)
    dtb = jnp.concatenate([dt_bias_f, dt_bias_b]).reshape(1, W)
    tr = _pick(Lp, (256,))
    spec = pl.BlockSpec((tr, W), lambda i: (i, 0))
    par = pl.BlockSpec((1, W), lambda i: (0, 0))
    return pl.pallas_call(
        functools.partial(_ssd_dt_body, L=L, tr=tr, nheads=W // 2),
        grid=(Lp // tr,),
        in_specs=[spec, par, par],
        out_specs=[spec, spec],
        out_shape=[jax.ShapeDtypeStruct((Lp, W), F32)] * 2,
        compiler_params=_cparams("parallel"),
        name="ssd_dt",
    )(raw, alog, dtb)


def _expand_heads(col, width):
    rows, R = col.shape
    return jnp.concatenate([jnp.broadcast_to(col[:, r:r + 1], (rows, width)) for r in range(R)], axis=1)


def _ssd_chunk_body(x_ref, b_ref, c_ref, scol_ref, srow_ref, o_ref, s_ref, *, reverse, nchunks, R):
    i = pl.program_id(1)

    @pl.when(i == 0)
    def _():
        s_ref[...] = jnp.zeros_like(s_ref)

    C = SSD_CHUNK
    P = SSD_HEAD_DIM
    rr = lax.broadcasted_iota(jnp.int32, (C, C), 0)
    cc = lax.broadcasted_iota(jnp.int32, (C, C), 1)
    incl = (rr <= cc) if reverse else (rr >= cc)
    last = 0 if reverse else C - 1
    chunks = range(nchunks)
    sls = [slice(c * C, (c + 1) * C) for c in chunks]
    dot = functools.partial(jnp.dot, preferred_element_type=F32)
    bm = [b_ref[sl, :] for sl in sls]
    cm = [c_ref[sl, :] for sl in sls]
    a_col = [scol_ref[sl, R:2 * R] for sl in sls]
    a_row = [srow_ref[:, sl] for sl in sls]
    a_last = [a[last:last + 1, :] for a in a_col]
    cb = [lax.dot_general(cm[c], bm[c], (((1,), (1,)), ((), ())), preferred_element_type=F32) for c in chunks]
    xdt = [x_ref[sls[c], :] * _expand_heads(scol_ref[sls[c], 0:R], P) for c in chunks]
    lmat = {(c, r): jnp.where(incl, jnp.exp(jnp.where(incl, a_col[c][:, r:r + 1] - a_row[c][r:r + 1, :], 0.0)), 0.0)
            for c in chunks for r in range(R)}
    for c in chunks:
        for r in range(R):
            o_ref[sls[c], r * P:(r + 1) * P] = dot(cb[c] * lmat[c, r], xdt[c][:, r * P:(r + 1) * P])
    st = [lax.dot_general(bm[c], xdt[c] * _expand_heads(jnp.exp(a_last[c] - a_col[c]), P),
                          (((0,), (0,)), ((), ())), preferred_element_type=F32) for c in chunks]
    S = s_ref[...]
    for c in (reversed(chunks) if reverse else chunks):
        o_ref[sls[c], :] += dot(cm[c], S) * _expand_heads(jnp.exp(a_col[c]), P)
        S = S * _expand_heads(jnp.exp(a_last[c]), P) + st[c]
    s_ref[...] = S


def _ssd_chunk(xbc, scol, srow, *, direction, d_inner):
    Lp = xbc.shape[0]
    G, N, P = SSD_GROUPS, SSD_STATE, SSD_HEAD_DIM
    R = d_inner // P // G
    gw = R * P
    tb = _pick(Lp, (SSD_ROWS_PER_STEP, 256))
    nblk = Lp // tb
    reverse = direction == 1
    b0 = d_inner // N
    c0 = b0 + G

    def blk(i):
        return nblk - 1 - i if reverse else i

    return pl.pallas_call(
        functools.partial(_ssd_chunk_body, reverse=reverse, nchunks=tb // SSD_CHUNK, R=R),
        grid=(G, nblk),
        in_specs=[pl.BlockSpec((tb, gw), lambda g, i: (blk(i), g)),
                  pl.BlockSpec((tb, N), lambda g, i: (blk(i), b0 + g)),
                  pl.BlockSpec((tb, N), lambda g, i: (blk(i), c0 + g)),
                  pl.BlockSpec((None, None, tb, 2 * R), lambda g, i: (direction, g, blk(i), 0)),
                  pl.BlockSpec((None, None, R, tb), lambda g, i: (direction, g, 0, blk(i)))],
        out_specs=pl.BlockSpec((tb, gw), lambda g, i: (blk(i), g)),
        out_shape=jax.ShapeDtypeStruct((Lp, d_inner), F32),
        scratch_shapes=[pltpu.VMEM((N, gw), F32)],
        compiler_params=_cparams("parallel", "arbitrary"),
        name="ssd_chunk_bwd" if reverse else "ssd_chunk_fwd",
    )(xbc, xbc, xbc, scol, srow)


def _ssd_norm_body(yf_ref, yb_ref, x_ref, z_ref, dsk_ref, nw_ref, o_ref):
    z = z_ref[...]
    y = (yf_ref[...] + yb_ref[...] + x_ref[...] * dsk_ref[...]) * (z * _sigmoid(z))
    y = y * lax.rsqrt(jnp.mean(y * y, axis=-1, keepdims=True) + RMS_EPS)
    o_ref[...] = (y * nw_ref[...]).astype(BF16)


def _ssd_norm(y_f, y_b, xbc, z, d_skip_row, norm_w):
    Lp, W = y_f.shape
    gw = W // SSD_GROUPS
    tr = _pick(Lp, (768, 512, 256))
    spec = pl.BlockSpec((tr, gw), lambda i, j: (i, j))
    par = pl.BlockSpec((1, gw), lambda i, j: (0, j))
    return pl.pallas_call(
        _ssd_norm_body,
        grid=(Lp // tr, SSD_GROUPS),
        in_specs=[spec, spec, spec, spec, par, par],
        out_specs=spec,
        out_shape=jax.ShapeDtypeStruct((Lp, W), BF16),
        compiler_params=_cparams("parallel", "parallel"),
        name="ssd_norm",
    )(y_f, y_b, xbc, z, d_skip_row, norm_w.reshape(1, W))


def _ssd_mixer(hb, h, L, w_in, conv_w, conv_b, a_log_f, a_log_b, dt_bias_f, dt_bias_b, d_skip, norm_w,
               w_out, ln_g, ln_b):
    Lp = hb.shape[0]
    d_inner = w_out.shape[0]
    conv_ch = conv_w.shape[1]
    nheads = d_inner // SSD_HEAD_DIM
    G = SSD_GROUPS
    R = nheads // G
    z = _proj(hb, w_in, 0, d_inner)
    xbc_pre = _proj(hb, w_in, d_inner, conv_ch)
    dt_raw = _proj(hb, w_in, d_inner + conv_ch, 2 * nheads)
    xbc = _conv(xbc_pre, conv_w, conv_b.reshape(1, conv_ch), L=L, col0=0, ncols=conv_ch, silu=True, l2norm=False)
    dt, acs = _ssd_dt(dt_raw, a_log_f, a_log_b, dt_bias_f, dt_bias_b, L=L)
    dt4 = jnp.transpose(dt.reshape(Lp, 2, G, R), (1, 2, 0, 3))
    acs4 = jnp.transpose(acs.reshape(Lp, 2, G, R), (1, 2, 0, 3))
    scol = jnp.concatenate([dt4, acs4], axis=-1)
    srow = jnp.transpose(acs4, (0, 1, 3, 2))
    y_f = _ssd_chunk(xbc, scol, srow, direction=0, d_inner=d_inner)
    y_b = _ssd_chunk(xbc, scol, srow, direction=1, d_inner=d_inner)
    d_skip_row = jnp.repeat(d_skip, SSD_HEAD_DIM).reshape(1, d_inner)
    a = _ssd_norm(y_f, y_b, xbc, z, d_skip_row, norm_w)
    return _out_ln(a, w_out, h, ln_g, ln_b)


def _fft_plan(L):
    N2 = _pick(L, (72, 64, 48, 32, 24, 16, 8))
    T1 = L // N2
    N1 = 2 * T1
    n = N1 * N2
    K1 = N1 // 2 + 1
    k1 = np.arange(K1, dtype=np.float64)[:, None]

    def stage1(T):
        th = 2.0 * np.pi * k1 * np.arange(T, dtype=np.float64)[None, :] / N1
        return np.cos(th), -np.sin(th)

    t2 = np.arange(N2, dtype=np.float64)[None, :]
    k2 = np.arange(N2, dtype=np.float64)[:, None]
    g = np.exp(-2j * np.pi * (t2 * k2 / N2)[None] - 2j * np.pi * (t2[None] * k1[:, :, None]) / n)
    gi = np.conj(np.transpose(g, (0, 2, 1)))

    def stack(m):
        return np.concatenate([np.concatenate([m.real, -m.imag], axis=2),
                               np.concatenate([m.imag, m.real], axis=2)], axis=1)

    t1 = np.arange(T1, dtype=np.float64)[:, None]
    kk = np.arange(K1, dtype=np.float64)[None, :]
    cw = np.where((kk == 0) | (kk == N1 // 2), 1.0, 2.0)
    th = 2.0 * np.pi * t1 * kk / N1
    f32 = lambda a: np.asarray(a, np.float32)
    fd_r, fd_i = stage1(T1)
    ff_r, ff_i = stage1(N1)
    return dict(N2=N2, T1=T1, N1=N1, K1=K1,
                fd_r=f32(fd_r), fd_i=f32(fd_i), ff_r=f32(ff_r), ff_i=f32(ff_i),
                g=f32(stack(g)), gi=f32(stack(gi)),
                fin_r=f32(cw * np.cos(th) / n), fin_i=f32(-cw * np.sin(th) / n))


SUBLANES = 8


def _dft1_body(fr_ref, fi_ref, x_ref, yr_ref, yi_ref):
    fr, fi = fr_ref[...], fi_ref[...]
    for s in range(SUBLANES):
        x = x_ref[:, s, :]
        yr_ref[:, s, :] = jnp.dot(fr, x, preferred_element_type=F32)
        yi_ref[:, s, :] = jnp.dot(fi, x, preferred_element_type=F32)


def _dft1(fr, fi, x3, col0, ncols):
    K1, T = fr.shape
    N2 = x3.shape[1]
    tc = _pick(ncols, (1024, 512, 256, 128))
    assert col0 % tc == 0 and N2 % SUBLANES == 0
    c0 = col0 // tc
    mat = pl.BlockSpec((K1, T), lambda j, c: (0, 0))
    out = pl.BlockSpec((K1, SUBLANES, tc), lambda j, c: (0, j, c))
    return pl.pallas_call(
        _dft1_body,
        grid=(N2 // SUBLANES, ncols // tc),
        in_specs=[mat, mat, pl.BlockSpec((T, SUBLANES, tc), lambda j, c: (0, j, c0 + c))],
        out_specs=[out, out],
        out_shape=[jax.ShapeDtypeStruct((K1, N2, ncols), F32)] * 2,
        compiler_params=_cparams("parallel", "parallel"),
        name="dft_stage1",
    )(fr, fi, x3)


def _dft2_body(g_ref, yr_ref, yi_ref, z_ref):
    y = jnp.concatenate([yr_ref[...], yi_ref[...]], axis=0)
    z_ref[...] = jnp.dot(g_ref[...], y, preferred_element_type=F32)


def _dft2(g, yr, yi, N2):
    K1 = g.shape[0]
    C = yr.shape[1]
    tc = _pick(C, (2048, 1024, 512, 256, 128))
    yspec = pl.BlockSpec((N2, tc), lambda a, j: (a, j))
    return pl.pallas_call(
        _dft2_body,
        grid=(K1, C // tc),
        in_specs=[pl.BlockSpec((None, 2 * N2, 2 * N2), lambda a, j: (a, 0, 0)), yspec, yspec],
        out_specs=pl.BlockSpec((2 * N2, tc), lambda a, j: (a, j)),
        out_shape=jax.ShapeDtypeStruct((K1 * 2 * N2, C), F32),
        compiler_params=_cparams("parallel", "parallel"),
        name="dft_stage2",
    )(g, yr, yi)


def _spec_mul_body(g_ref, gi_ref, yr_ref, yi_ref, h_ref, qr_ref, qi_ref, *, N2):
    y = jnp.concatenate([yr_ref[...], yi_ref[...]], axis=0)
    z = jnp.dot(g_ref[...], y, preferred_element_type=F32)
    zr, zi = z[:N2], z[N2:]
    hr, hi = h_ref[0:N2, :], h_ref[N2:2 * N2, :]
    p = jnp.concatenate([zr * hr - zi * hi, zr * hi + zi * hr], axis=0)
    q = jnp.dot(gi_ref[...], p, preferred_element_type=F32)
    qr_ref[...] = q[:N2]
    qi_ref[...] = q[N2:]


def _spec_mul(g, gi, yr, yi, hspec, order, N2):
    K1 = g.shape[0]
    C = yr.shape[1]
    mat = pl.BlockSpec((None, 2 * N2, 2 * N2), lambda a: (a, 0, 0))
    yspec = pl.BlockSpec((N2, C), lambda a: (a, 0))
    return pl.pallas_call(
        functools.partial(_spec_mul_body, N2=N2),
        grid=(K1,),
        in_specs=[mat, mat, yspec, yspec, pl.BlockSpec((2 * N2, C), lambda a: (a, order))],
        out_specs=[yspec, yspec],
        out_shape=[jax.ShapeDtypeStruct((K1 * N2, C), F32)] * 2,
        compiler_params=_cparams("parallel"),
        name="spec_mul",
    )(g, gi, yr, yi, hspec)


def _idft_gate_body(fr_ref, fi_ref, qr_ref, qi_ref, z_ref, gate_ref, skip_ref, o_ref):
    fr, fi = fr_ref[...], fi_ref[...]
    for s in range(SUBLANES):
        y = jnp.dot(fr, qr_ref[:, s, :], preferred_element_type=F32)
        y = y + jnp.dot(fi, qi_ref[:, s, :], preferred_element_type=F32)
        o_ref[:, s, :] = (gate_ref[:, s, :] * (y + z_ref[:, s, :] * skip_ref[...])).astype(o_ref.dtype)


def _idft_gate(fr, fi, qr3, qi3, z3, zcol, gate3, gcol, skip, out_dtype):
    T1, K1 = fr.shape
    N2, C = qr3.shape[1], qr3.shape[2]
    tc = _pick(C, (1024, 512, 256, 128))
    assert zcol % tc == 0 and gcol % tc == 0
    mat = pl.BlockSpec((T1, K1), lambda j, c: (0, 0))
    qs = pl.BlockSpec((K1, SUBLANES, tc), lambda j, c: (0, j, c))
    return pl.pallas_call(
        _idft_gate_body,
        grid=(N2 // SUBLANES, C // tc),
        in_specs=[mat, mat, qs, qs,
                  pl.BlockSpec((T1, SUBLANES, tc), lambda j, c: (0, j, zcol // tc + c)),
                  pl.BlockSpec((T1, SUBLANES, tc), lambda j, c: (0, j, gcol // tc + c)),
                  pl.BlockSpec((1, tc), lambda j, c: (0, c))],
        out_specs=pl.BlockSpec((T1, SUBLANES, tc), lambda j, c: (0, j, c)),
        out_shape=jax.ShapeDtypeStruct((T1, N2, C), out_dtype),
        compiler_params=_cparams("parallel", "parallel"),
        name="idft_gate",
    )(fr, fi, qr3, qi3, z3, gate3, skip.reshape(1, C))


def _hy_filter_body(feat_ref, w1_ref, b1_ref, f1_ref, w2_ref, b2_ref, f2_ref, w3_ref, b3_ref, dl_ref, o_ref,
                    *, L, tr):
    i = pl.program_id(0)
    feats = feat_ref[...]
    z = jnp.sin(f1_ref[...] * (jnp.dot(feats, w1_ref[...], preferred_element_type=F32) + b1_ref[...]))
    z = jnp.sin(f2_ref[...] * (jnp.dot(z, w2_ref[...], preferred_element_type=F32) + b2_ref[...]))
    r = jnp.dot(z, w3_ref[...], preferred_element_type=F32) + b3_ref[...]
    r = r * jnp.exp(-feats[:, 0:1] * dl_ref[...])
    o_ref[...] = jnp.where(_row_ids(i, tr) == L, 0.0, r)


def _hy_filter(L, fw1, fb1, freq1, fw2, fb2, freq2, fw3, fb3, width):
    pos = np.arange(2 * L)
    pos = np.where(pos < L, pos, (2 * L - pos) % L).astype(np.float64)
    tt = pos / (L - 1)
    w = 2.0 * np.pi * pos / L
    f = np.linspace(1e-4, HY_BANDS - 1, HY_BANDS)
    feats = np.concatenate([tt[:, None], np.cos(f[None] * w[:, None]), -np.sin(f[None] * w[:, None])], axis=1)
    feats = jnp.asarray(feats, F32)
    emb = feats.shape[1]
    hid = fw1.shape[1]
    n_ord = fw3.shape[1] // (2 * width)
    oc = n_ord * width
    deltas = np.abs(np.linspace(math.log(HY_TARGET) / HY_DECAY_PCT_SHORT,
                                math.log(HY_TARGET) / HY_DECAY_PCT_LONG, width))
    dl = jnp.asarray(np.tile(deltas, n_ord)[None], F32)
    tr = max(d for d in range(8, min(L, 512) + 1, 8) if L % d == 0)
    nt = L // tr
    row = lambda a: a.reshape(1, -1)
    full = lambda s: pl.BlockSpec(s, lambda i: (0, 0))
    return pl.pallas_call(
        functools.partial(_hy_filter_body, L=L, tr=tr),
        grid=(2 * nt,),
        in_specs=[pl.BlockSpec((tr, emb), lambda i: (i, 0)),
                  full((emb, hid)), full((1, hid)), full((1, hid)),
                  full((hid, hid)), full((1, hid)), full((1, hid)),
                  pl.BlockSpec((hid, oc), lambda i: (0, i // nt)),
                  pl.BlockSpec((1, oc), lambda i: (0, i // nt)),
                  full((1, oc))],
        out_specs=pl.BlockSpec((tr, oc), lambda i: (i, 0)),
        out_shape=jax.ShapeDtypeStruct((2 * L, oc), F32),
        compiler_params=_cparams("parallel"),
        name="hy_filter",
    )(feats, fw1, row(fb1), row(freq1), fw2, row(fb2), row(freq2), fw3, row(fb3), dl)


def _hyena_mixer(hb, h, L, w_in, conv_w, conv_b, fw1, fb1, freq1, fw2, fb2, freq2, fw3, fb3, filt_skip, w_out,
                 ln_g, ln_b):
    Lp = hb.shape[0]
    W = w_out.shape[0]
    n_ord = filt_skip.shape[0]
    u_pre = _proj(hb, w_in, 0, 3 * W)
    u = _conv(u_pre, conv_w, conv_b.reshape(1, 3 * W), L=L, col0=0, ncols=3 * W, silu=False, l2norm=False,
              out_rows=L)
    plan = _fft_plan(L)
    N2, T1, N1, K1 = plan["N2"], plan["T1"], plan["N1"], plan["K1"]
    cst = {k: jnp.asarray(v) for k, v in plan.items() if isinstance(v, np.ndarray)}
    resp = _hy_filter(L, fw1, fb1, freq1, fw2, fb2, freq2, fw3, fb3, W)
    hr, hi = _dft1(cst["ff_r"], cst["ff_i"], resp.reshape(N1, N2, n_ord * W), 0, n_ord * W)
    hspec = _dft2(cst["g"], hr.reshape(K1 * N2, n_ord * W), hi.reshape(K1 * N2, n_ord * W), N2)
    u3 = u.reshape(T1, N2, 3 * W)
    z3, zcol = u3, n_ord * W
    for o in range(n_ord):
        yr, yi = _dft1(cst["fd_r"], cst["fd_i"], z3, zcol, W)
        qr, qi = _spec_mul(cst["g"], cst["gi"], yr.reshape(K1 * N2, W), yi.reshape(K1 * N2, W), hspec, o, N2)
        z3 = _idft_gate(cst["fin_r"], cst["fin_i"], qr.reshape(K1, N2, W), qi.reshape(K1, N2, W), z3, zcol,
                        u3, o * W, filt_skip[o], F32)
        zcol = 0
    return _out_ln(z3.reshape(L, W), w_out, h, ln_g, ln_b, rows=L)


def _router_body(h_ref, w_ref, b_ref, o_ref, cnt_ref, base_ref, *, L):
    i = pl.program_id(0)

    @pl.when(i == 0)
    def _():
        base_ref[...] = jnp.zeros_like(base_ref)

    logits = jnp.dot(h_ref[...], w_ref[...], precision=HIGHEST, preferred_element_type=F32) + b_ref[...]
    tm = logits.shape[0]
    lane = lax.broadcasted_iota(jnp.int32, (tm, LANES), 1)
    neg = -jnp.inf
    G, E, PG = MOE_GROUPS, MOE_EXPERTS, MOE_PER_GROUP
    glog = jnp.where(lane < G, logits, neg)
    gmax = jnp.max(glog, axis=-1, keepdims=True)
    gsel = jnp.min(jnp.where(glog == gmax, lane, LANES), axis=-1, keepdims=True)
    gw = 1.0 / jnp.sum(jnp.exp(glog - gmax), axis=-1, keepdims=True)
    lo = G + gsel * PG
    elog = jnp.where(jnp.logical_and(lane >= lo, lane < lo + PG), logits, neg)
    m1 = jnp.max(elog, axis=-1, keepdims=True)
    i1 = jnp.min(jnp.where(elog == m1, lane, LANES), axis=-1, keepdims=True)
    elog2 = jnp.where(lane == i1, neg, elog)
    m2 = jnp.max(elog2, axis=-1, keepdims=True)
    i2 = jnp.min(jnp.where(elog2 == m2, lane, LANES), axis=-1, keepdims=True)
    e2 = jnp.exp(m2 - m1)
    w1 = gw / (1.0 + e2)
    w2 = gw * e2 / (1.0 + e2)
    valid = _row_ids(i, tm) < L
    oh1 = jnp.where(jnp.logical_and(valid, lane == i1 - G), 1.0, 0.0)
    oh2 = jnp.where(jnp.logical_and(valid, lane == i2 - G), 1.0, 0.0)
    both = oh1 + oh2
    rr = lax.broadcasted_iota(jnp.int32, (tm, tm), 0)
    cc = lax.broadcasted_iota(jnp.int32, (tm, tm), 1)
    earlier = jnp.where(rr > cc, 1.0, 0.0).astype(BF16)
    before = base_ref[...] + jnp.dot(earlier, both.astype(BF16), preferred_element_type=F32)
    rank1 = jnp.sum(before * oh1, axis=-1, keepdims=True)
    rank2 = jnp.sum(before * oh2, axis=-1, keepdims=True)
    base_ref[...] += jnp.sum(both, axis=0, keepdims=True)
    cnt_ref[...] = base_ref[...]
    cols = [(i1 - G).astype(F32), (i2 - G).astype(F32), w1, w2, rank1, rank2]
    out = jnp.zeros((tm, LANES), F32)
    for n, col in enumerate(cols):
        out = jnp.where(lane == n, col, out)
    o_ref[...] = out


def _router(h, wr_group, br_group, wr_expert, br_expert, *, L):
    Lp, D = h.shape
    used = MOE_GROUPS + MOE_EXPERTS
    w = jnp.pad(jnp.concatenate([wr_group, wr_expert], axis=1), ((0, 0), (0, LANES - used)))
    b = jnp.pad(jnp.concatenate([br_group, br_expert]), (0, LANES - used)).reshape(1, LANES)
    tm = _pick(Lp, (256,))
    return pl.pallas_call(
        functools.partial(_router_body, L=L),
        grid=(Lp // tm,),
        in_specs=[pl.BlockSpec((tm, D), lambda i: (i, 0)),
                  pl.BlockSpec((D, LANES), lambda i: (0, 0)),
                  pl.BlockSpec((1, LANES), lambda i: (0, 0))],
        out_specs=[pl.BlockSpec((tm, LANES), lambda i: (i, 0)),
                   pl.BlockSpec((1, LANES), lambda i: (0, 0))],
        out_shape=[jax.ShapeDtypeStruct((Lp, LANES), F32), jax.ShapeDtypeStruct((1, LANES), F32)],
        scratch_shapes=[pltpu.VMEM((1, LANES), F32)],
        compiler_params=_cparams("arbitrary"),
        name="router",
    )(h, w, b)


def _row_copy(src_hbm, dst_vmem, sem, src_row, dst_row):
    return pltpu.make_async_copy(src_hbm.at[pl.ds(src_row, 1)], dst_vmem.at[pl.ds(dst_row, 1)], sem)


def _gather_start(idx_ref, base, src_hbm, dst_vmem, sem, rows):
    def issue(s, carry):
        _row_copy(src_hbm, dst_vmem, sem, idx_ref[base + s], s).start()
        return carry

    lax.fori_loop(0, rows, issue, 0)


def _gather_wait(src_hbm, dst_vmem, sem, rows):
    pltpu.make_async_copy(src_hbm.at[pl.ds(0, rows)], dst_vmem.at[pl.ds(0, rows)], sem).wait()


def _expert_body(tok_ref, be_ref, first_ref, nu_ref, h_hbm, wg_ref, wu_ref, wd_ref, o_ref,
                 xbuf, wgb, wub, wdb, sem, *, blk):
    b = pl.program_id(0)
    n_used = nu_ref[0]
    slot = b % 2

    @pl.when(jnp.logical_and(b == 0, n_used > 0))
    def _():
        _gather_start(tok_ref, 0, h_hbm, xbuf.at[0], sem.at[0], blk)

    @pl.when(b + 1 < n_used)
    def _():
        _gather_start(tok_ref, (b + 1) * blk, h_hbm, xbuf.at[1 - slot], sem.at[1 - slot], blk)

    @pl.when(b < n_used)
    def _():
        @pl.when(first_ref[b] == 1)
        def _():
            wgb[...] = wg_ref[...].astype(BF16)
            wub[...] = wu_ref[...].astype(BF16)
            wdb[...] = wd_ref[...].astype(BF16)

        _gather_wait(h_hbm, xbuf.at[slot], sem.at[slot], blk)
        x = xbuf[slot].astype(BF16)
        g = jnp.dot(x, wgb[...], preferred_element_type=F32)
        u = jnp.dot(x, wub[...], preferred_element_type=F32)
        hid = (g * _sigmoid(g)) * u
        o_ref[...] = jnp.dot(hid.astype(BF16), wdb[...], preferred_element_type=F32)

    @pl.when(b >= n_used)
    def _():
        o_ref[...] = jnp.zeros_like(o_ref)


def _experts(h, slot_tok, blk_expert, first, n_used, wg, wu, wd):
    D = h.shape[1]
    FF = wg.shape[2]
    BLK = MOE_BLOCK
    nb = slot_tok.shape[0] // BLK
    wspec = lambda s: pl.BlockSpec((None,) + s, lambda b, tok, be, fi, nu: (be[b], 0, 0))
    return pl.pallas_call(
        functools.partial(_expert_body, blk=BLK),
        grid_spec=pltpu.PrefetchScalarGridSpec(
            num_scalar_prefetch=4,
            grid=(nb,),
            in_specs=[pl.BlockSpec(memory_space=pl.ANY), wspec((D, FF)), wspec((D, FF)), wspec((FF, D))],
            out_specs=pl.BlockSpec((BLK, D), lambda b, tok, be, fi, nu: (b, 0)),
            scratch_shapes=[pltpu.VMEM((2, BLK, D), F32),
                            pltpu.VMEM((D, FF), BF16), pltpu.VMEM((D, FF), BF16), pltpu.VMEM((FF, D), BF16),
                            pltpu.SemaphoreType.DMA((2,))]),
        out_shape=jax.ShapeDtypeStruct((nb * BLK, D), F32),
        compiler_params=_cparams("arbitrary"),
        name="moe_experts",
    )(slot_tok, blk_expert, first, n_used, h, wg, wu, wd)


def _combine_body(d0_ref, d1_ref, yb_hbm, rw_ref, h_ref, g_ref, b_ref, o_ref, ob_ref, buf, sem, *, L, tm):
    i = pl.program_id(0)
    n = pl.num_programs(0)
    slot = i % 2

    def start(tile, sl):
        _gather_start(d0_ref, tile * tm, yb_hbm, buf.at[sl, 0], sem.at[sl], tm)
        _gather_start(d1_ref, tile * tm, yb_hbm, buf.at[sl, 1], sem.at[sl], tm)

    @pl.when(i == 0)
    def _():
        start(0, 0)

    @pl.when(i + 1 < n)
    def _():
        start(i + 1, 1 - slot)

    _gather_wait(yb_hbm, buf.at[slot, 0], sem.at[slot], tm)
    _gather_wait(yb_hbm, buf.at[slot, 1], sem.at[slot], tm)
    rw = rw_ref[...]
    f = buf[slot, 0] * rw[:, 2:3] + buf[slot, 1] * rw[:, 3:4]
    f = jnp.where(_row_ids(i, tm) < L, f, 0.0)
    y = _layer_norm_rows(DN_ALPHA * h_ref[...] + f, g_ref[...], b_ref[...])
    o_ref[...] = y
    ob_ref[...] = y.astype(BF16)


def _combine_ln(yb, dest0, dest1, rw, h, g, b, *, L):
    Lp, D = h.shape
    tm = _pick(Lp, (256,))
    row = pl.BlockSpec((tm, D), lambda i, d0, d1: (i, 0))
    par = pl.BlockSpec((1, D), lambda i, d0, d1: (0, 0))
    return pl.pallas_call(
        functools.partial(_combine_body, L=L, tm=tm),
        grid_spec=pltpu.PrefetchScalarGridSpec(
            num_scalar_prefetch=2,
            grid=(Lp // tm,),
            in_specs=[pl.BlockSpec(memory_space=pl.ANY),
                      pl.BlockSpec((tm, LANES), lambda i, d0, d1: (i, 0)),
                      row, par, par],
            out_specs=[row, row],
            scratch_shapes=[pltpu.VMEM((2, 2, tm, D), F32), pltpu.SemaphoreType.DMA((2,))]),
        out_shape=[jax.ShapeDtypeStruct((Lp, D), F32), jax.ShapeDtypeStruct((Lp, D), BF16)],
        compiler_params=_cparams("arbitrary"),
        name="moe_combine",
    )(dest0, dest1, yb, rw, h, g.reshape(1, D), b.reshape(1, D))


def _moe(h, L, wr_group, br_group, wr_expert, br_expert, w_gate, w_up, w_down, ln_g, ln_b):
    Lp, D = h.shape
    E, BLK = MOE_EXPERTS, MOE_BLOCK
    rw, cnt = _router(h, wr_group, br_group, wr_expert, br_expert, L=L)
    counts = cnt[0, :E].astype(jnp.int32)
    padded = (counts + BLK - 1) // BLK * BLK
    pend = jnp.cumsum(padded)
    pstart = pend - padded
    eid = rw[:, 0:2].astype(jnp.int32)
    dest = pstart[eid] + rw[:, 4:6].astype(jnp.int32)
    tok = lax.broadcasted_iota(jnp.int32, (Lp, 2), 0)
    dest = jnp.where(tok < L, dest, 0)
    nb = -(-2 * L // BLK) + E
    slot_tok = jnp.full((nb * BLK,), L, jnp.int32).at[dest[:L].reshape(-1)].set(tok[:L].reshape(-1))
    blk_expert = jnp.minimum(jnp.searchsorted(pend, jnp.arange(nb, dtype=jnp.int32) * BLK, side="right"),
                             E - 1).astype(jnp.int32)
    first = jnp.concatenate([jnp.ones((1,), jnp.int32), (blk_expert[1:] != blk_expert[:-1]).astype(jnp.int32)])
    n_used = (pend[-1:] // BLK).astype(jnp.int32)
    yb = _experts(h, slot_tok, blk_expert, first, n_used, w_gate, w_up, w_down)
    return _combine_ln(yb, dest[:, 0], dest[:, 1], rw, h, ln_g, ln_b, L=L)


def kernel(x, meta_tokens, ln_mix_g, ln_mix_b, ln_ffn_g, ln_ffn_b, gdn_w_in, gdn_conv_w, gdn_a_log_f, gdn_a_log_b, gdn_dt_bias_f, gdn_dt_bias_b, gdn_norm_w, gdn_w_out, ssd_w_in, ssd_conv_w, ssd_conv_b, ssd_a_log_f, ssd_a_log_b, ssd_dt_bias_f, ssd_dt_bias_b, ssd_d_skip, ssd_norm_w, ssd_w_out, hy_w_in, hy_conv_w, hy_conv_b, hy_fw1, hy_fb1, hy_freq1, hy_fw2, hy_fb2, hy_freq2, hy_fw3, hy_fb3, hy_filt_skip, hy_w_out, moe_wr_group, moe_br_group, moe_wr_expert, moe_br_expert, moe_w_gate, moe_w_up, moe_w_down):
    assert x.shape[0] == 1
    n_meta = meta_tokens.shape[0]
    L = n_meta + x.shape[1]
    Lp = -(-L // 256) * 256
    depth = ln_mix_g.shape[0]
    h = jnp.concatenate([meta_tokens.astype(x.dtype), x[0]], axis=0)
    h = jnp.pad(h, ((0, Lp - L), (0, 0)))
    hb = h.astype(BF16)
    for i in range(depth):
        kind, j = i % N_MIXERS, i // N_MIXERS
        if kind == 0:
            h, hb = _gdn_mixer(hb, h, L, gdn_w_in[j], gdn_conv_w[j], gdn_a_log_f[j], gdn_a_log_b[j],
                               gdn_dt_bias_f[j], gdn_dt_bias_b[j], gdn_norm_w[j], gdn_w_out[j],
                               ln_mix_g[i], ln_mix_b[i])
        elif kind == 1:
            h, hb = _ssd_mixer(hb, h, L, ssd_w_in[j], ssd_conv_w[j], ssd_conv_b[j], ssd_a_log_f[j],
                               ssd_a_log_b[j], ssd_dt_bias_f[j], ssd_dt_bias_b[j], ssd_d_skip[j],
                               ssd_norm_w[j], ssd_w_out[j], ln_mix_g[i], ln_mix_b[i])
        else:
            h, hb = _hyena_mixer(hb, h, L, hy_w_in[j], hy_conv_w[j], hy_conv_b[j], hy_fw1[j], hy_fb1[j],
                                 hy_freq1[j], hy_fw2[j], hy_fb2[j], hy_freq2[j], hy_fw3[j], hy_fb3[j],
                                 hy_filt_skip[j], hy_w_out[j], ln_mix_g[i], ln_mix_b[i])
        h, hb = _moe(h, L, moe_wr_group[i], moe_br_group[i], moe_wr_expert[i], moe_br_expert[i],
                     moe_w_gate[i], moe_w_up[i], moe_w_down[i], ln_ffn_g[i], ln_ffn_b[i])
    return h[n_meta:L][None]
```

```python
import functools
import math

import numpy as np
import jax
import jax.numpy as jnp
from jax import lax
from jax.experimental import pallas as pl
from jax.experimental.pallas import tpu as pltpu

F32 = jnp.float32
BF16 = jnp.bfloat16
HIGHEST = lax.Precision.HIGHEST

LANES = 128
VMEM_LIMIT = 56 * 1024 * 1024

DEPTH = 4
N_MIXERS = 3
DN_ALPHA = (2.0 * DEPTH) ** 0.25
LN_EPS = 1e-5
RMS_EPS = 1e-6

GDN_QK_HEADS = 16
GDN_V_HEADS = 32
GDN_HEAD_DIM = 128
GDN_QK_DIM = GDN_QK_HEADS * GDN_HEAD_DIM
GDN_V_DIM = GDN_V_HEADS * GDN_HEAD_DIM
GDN_CONV_CH = 2 * GDN_QK_DIM + GDN_V_DIM
GDN_CHUNK = 64
GDN_ROWS_PER_STEP = 256
GDN_QK_HEADS_PER_STEP = 2

SSD_HEAD_DIM = 64
SSD_GROUPS = 8
SSD_STATE = 128
SSD_CHUNK = 128
SSD_ROWS_PER_STEP = 256

HY_BANDS = 16
HY_TARGET = 1e-2
HY_DECAY_PCT_SHORT = 0.3
HY_DECAY_PCT_LONG = 1.5

MOE_GROUPS = 4
MOE_PER_GROUP = 8
MOE_EXPERTS = MOE_GROUPS * MOE_PER_GROUP
MOE_BLOCK = 128


def _cparams(*sem):
    return pltpu.CompilerParams(dimension_semantics=sem, vmem_limit_bytes=VMEM_LIMIT)


def _pick(n, candidates):
    for c in candidates:
        if n % c == 0:
            return c
    raise ValueError(f"no tile in {candidates} divides {n}")


def _row_ids(i, tr):
    return i * tr + lax.broadcasted_iota(jnp.int32, (tr, 1), 0)


def _layer_norm_rows(y, g, b):
    mu = jnp.mean(y, axis=-1, keepdims=True)
    d = y - mu
    var = jnp.mean(d * d, axis=-1, keepdims=True)
    return d * lax.rsqrt(var + LN_EPS) * g + b


def _softplus(x):
    return jnp.maximum(x, 0.0) + jnp.log(1.0 + jnp.exp(-jnp.abs(x)))


def _sigmoid(x):
    return 1.0 / (1.0 + jnp.exp(-x))


def _proj_body(a_ref, w_ref, o_ref):
    o_ref[...] = jnp.dot(a_ref[...], w_ref[...].astype(BF16), preferred_element_type=F32)


def _proj(a, w, col0, ncols):
    Lp, K = a.shape
    w_all, layer = w
    tn = _pick(ncols, (1024, 512, 256, 128))
    assert col0 % tn == 0
    tm = _pick(Lp, (1408, 1056, 768, 512, 256))
    return pl.pallas_call(
        _proj_body,
        grid=(ncols // tn, Lp // tm),
        in_specs=[pl.BlockSpec((tm, K), lambda j, i: (i, 0)),
                  pl.BlockSpec((None, K, tn), lambda j, i: (layer, 0, col0 // tn + j))],
        out_specs=pl.BlockSpec((tm, tn), lambda j, i: (i, j)),
        out_shape=jax.ShapeDtypeStruct((Lp, ncols), F32),
        compiler_params=_cparams("parallel", "parallel"),
        name="proj",
    )(a, w_all)


def _store_row_tiles(o3_ref, y):
    for s in range(y.shape[1] // LANES):
        o3_ref[:, s, :] = y[:, s * LANES:(s + 1) * LANES]


def _load_row_tiles(x3_ref, s):
    return x3_ref[:, s, :]


def _out_ln_body(a_ref, w_ref, h_ref, g_ref, b_ref, o_ref, o3_ref, acc_ref, *, nk, rows):
    i = pl.program_id(0)
    k = pl.program_id(1)

    @pl.when(k == 0)
    def _():
        acc_ref[...] = jnp.zeros_like(acc_ref)

    a = a_ref[...]
    if rows is not None:
        a = jnp.where(_row_ids(i, a.shape[0]) < rows, a, 0.0)
    acc_ref[...] += jnp.dot(a.astype(BF16), w_ref[...].astype(BF16), preferred_element_type=F32)

    @pl.when(k == nk - 1)
    def _():
        y = _layer_norm_rows(DN_ALPHA * h_ref[...] + acc_ref[...], g_ref[...], b_ref[...])
        o_ref[...] = y
        _store_row_tiles(o3_ref, y)


def _out_ln(a, w, h, g, b, rows=None):
    Lp = h.shape[0]
    K = a.shape[1]
    assert a.shape[0] == (Lp if rows is None else rows)
    w_all, layer = w
    D = w_all.shape[2]
    tm = _pick(Lp, (704, 512, 256))
    tk = _pick(K, (512, 256))
    nk = K // tk
    return pl.pallas_call(
        functools.partial(_out_ln_body, nk=nk, rows=rows),
        grid=(Lp // tm, nk),
        in_specs=[pl.BlockSpec((tm, tk), lambda i, k: (i, k)),
                  pl.BlockSpec((None, tk, D), lambda i, k: (layer, k, 0)),
                  pl.BlockSpec((tm, D), lambda i, k: (i, 0)),
                  pl.BlockSpec((1, D), lambda i, k: (0, 0)),
                  pl.BlockSpec((1, D), lambda i, k: (0, 0))],
        out_specs=[pl.BlockSpec((tm, D), lambda i, k: (i, 0)),
                   pl.BlockSpec((tm, D // LANES, LANES), lambda i, k: (i, 0, 0))],
        out_shape=[jax.ShapeDtypeStruct((Lp, D), F32), jax.ShapeDtypeStruct((Lp, D // LANES, LANES), F32)],
        scratch_shapes=[pltpu.VMEM((tm, D), F32)],
        compiler_params=_cparams("parallel", "arbitrary"),
        name="out_ln",
    )(a, w_all, h, g.reshape(1, D), b.reshape(1, D))


def _conv_body(prev_ref, cur_ref, next_ref, w_ref, b_ref, o_ref, *, taps, L, tr, nblk, silu, l2norm):
    i = pl.program_id(0)
    half = taps // 2
    r8 = lax.broadcasted_iota(jnp.int32, (8, 1), 0)
    cur = jnp.where(_row_ids(i, tr) < L, cur_ref[...], 0.0)
    prev = jnp.where(jnp.logical_and(i > 0, i * tr - 8 + r8 < L), prev_ref[...], 0.0)
    nxt = jnp.where(jnp.logical_and(i < nblk - 1, (i + 1) * tr + r8 < L), next_ref[...], 0.0)
    ext = jnp.concatenate([prev, cur, nxt], axis=0)
    n = tr + 16
    acc = None
    for k in range(taps):
        d = k - half
        win = cur if d == 0 else pltpu.roll(ext, (n - d) % n, axis=0)[8:8 + tr]
        term = win * w_ref[k:k + 1, :]
        acc = term if acc is None else acc + term
    acc = acc + b_ref[...]
    if silu:
        acc = acc * _sigmoid(acc)
    acc = jnp.where(_row_ids(i, tr) < L, acc, 0.0)
    if l2norm:
        tc = acc.shape[1]
        for hh in range(tc // LANES):
            blk = acc[:, hh * LANES:(hh + 1) * LANES]
            ss = jnp.sum(blk * blk, axis=-1, keepdims=True)
            o_ref[:, hh * LANES:(hh + 1) * LANES] = blk * lax.rsqrt(ss + 1e-6)
    else:
        o_ref[...] = acc


def _conv(pre, w, b, *, L, col0, ncols, silu, l2norm, out_rows=None):
    Lp = pre.shape[0]
    out_rows = Lp if out_rows is None else out_rows
    taps = w.shape[0]
    tr = max(d for d in range(8, 769, 8) if out_rows % d == 0)
    tc = _pick(ncols, (512, 256, 128))
    assert col0 % tc == 0
    nblk = out_rows // tr
    c0 = col0 // tc
    r8 = tr // 8
    last8 = Lp // 8 - 1
    return pl.pallas_call(
        functools.partial(_conv_body, taps=taps, L=L, tr=tr, nblk=nblk, silu=silu, l2norm=l2norm),
        grid=(nblk, ncols // tc),
        in_specs=[pl.BlockSpec((8, tc), lambda i, j: (jnp.maximum(i * r8 - 1, 0), c0 + j)),
                  pl.BlockSpec((tr, tc), lambda i, j: (i, c0 + j)),
                  pl.BlockSpec((8, tc), lambda i, j: (jnp.minimum((i + 1) * r8, last8), c0 + j)),
                  pl.BlockSpec((taps, tc), lambda i, j: (0, c0 + j)),
                  pl.BlockSpec((1, tc), lambda i, j: (0, c0 + j))],
        out_specs=pl.BlockSpec((tr, tc), lambda i, j: (i, j)),
        out_shape=jax.ShapeDtypeStruct((out_rows, ncols), F32),
        compiler_params=_cparams("parallel", "parallel"),
        name="conv",
    )(pre, pre, pre, w, b)


def _tri(n, upper):
    r = lax.broadcasted_iota(jnp.int32, (n, n), 0)
    c = lax.broadcasted_iota(jnp.int32, (n, n), 1)
    return ((r <= c) if upper else (r >= c)).astype(F32)


def _gdn_gates_body(raw_ref, alog_ref, dtb_ref, o_ref, *, L, tr):
    i = pl.program_id(0)
    nh = GDN_V_HEADS
    lane = lax.broadcasted_iota(jnp.int32, (1, 4 * nh), 1)
    is_g = (lane // nh) % 2 == 0
    raw = raw_ref[...]
    g = -jnp.exp(alog_ref[...]) * _softplus(raw + dtb_ref[...])
    val = jnp.where(is_g, g, _sigmoid(raw))
    val = jnp.where(_row_ids(i, tr) < L, val, 0.0)
    C = GDN_CHUNK
    lo, up = _tri(C, False), _tri(C, True)
    for c in range(tr // C):
        blk = val[c * C:(c + 1) * C]
        pre = jnp.dot(lo, blk, precision=HIGHEST, preferred_element_type=F32)
        suf = jnp.dot(up, blk, precision=HIGHEST, preferred_element_type=F32)
        o_ref[c * C:(c + 1) * C, :] = jnp.where(lane < nh, pre, jnp.where(lane // nh == 2, suf, blk))


def _gdn_gates(raw, a_log_f, a_log_b, dt_bias_f, dt_bias_b, *, L):
    Lp, W = raw.shape
    zeros = jnp.zeros_like(a_log_f)
    alog = jnp.concatenate([a_log_f, zeros, a_log_b, zeros]).reshape(1, W)
    dtb = jnp.concatenate([dt_bias_f, zeros, dt_bias_b, zeros]).reshape(1, W)
    tr = _pick(Lp, (256,))
    return pl.pallas_call(
        functools.partial(_gdn_gates_body, L=L, tr=tr),
        grid=(Lp // tr,),
        in_specs=[pl.BlockSpec((tr, W), lambda i: (i, 0)),
                  pl.BlockSpec((1, W), lambda i: (0, 0)),
                  pl.BlockSpec((1, W), lambda i: (0, 0))],
        out_specs=pl.BlockSpec((tr, W), lambda i: (i, 0)),
        out_shape=jax.ShapeDtypeStruct((Lp, W), F32),
        compiler_params=_cparams("parallel"),
        name="gdn_gates",
    )(raw, alog, dtb)


def _gdn_chunk_body(q_ref, k_ref, v_ref, gcol_ref, grow_ref, o_ref, s_ref, qm_ref, on_ref, *, reverse, nchunks,
                    nqk):
    i = pl.program_id(1)

    @pl.when(i == 0)
    def _():
        s_ref[...] = jnp.zeros_like(s_ref)

    C = GDN_CHUNK
    Dh = GDN_HEAD_DIM
    rr = lax.broadcasted_iota(jnp.int32, (C, C), 0)
    cc = lax.broadcasted_iota(jnp.int32, (C, C), 1)
    incl = (rr <= cc) if reverse else (rr >= cc)
    strict = (rr < cc) if reverse else (rr > cc)
    eye = (rr == cc).astype(F32)
    scale = Dh ** -0.5
    last = 0 if reverse else C - 1
    chunks = range(nchunks)
    heads = range(2 * nqk)
    chains = [(c, hv) for c in chunks for hv in heads]
    sls = [slice(c * C, (c + 1) * C) for c in chunks]
    hsl = [slice(hv * Dh, (hv + 1) * Dh) for hv in heads]
    dot = functools.partial(jnp.dot, preferred_element_type=F32)

    kc = {(c, hq): k_ref[sls[c], hsl[hq]] for c in chunks for hq in range(nqk)}
    qc = {(c, hq): q_ref[sls[c], hsl[hq]] * scale for c in chunks for hq in range(nqk)}
    g_col = {(c, hv): gcol_ref[hv // 2, sls[c], hv % 2:hv % 2 + 1] for c, hv in chains}
    b_col = {(c, hv): gcol_ref[hv // 2, sls[c], 2 + hv % 2:3 + hv % 2] for c, hv in chains}
    g_row = {(c, hv): grow_ref[hv // 2, hv % 2:hv % 2 + 1, sls[c]] for c, hv in chains}
    g_last = {ch: g_row[ch][:, last:last + 1] for ch in chains}
    kb = {(c, hv): kc[c, hv // 2] * b_col[c, hv] for c, hv in chains}
    kk = {(c, hq): lax.dot_general(jnp.concatenate([kb[c, 2 * hq], kb[c, 2 * hq + 1], qc[c, hq]], axis=0),
                                   kc[c, hq], (((1,), (1,)), ((), ())), preferred_element_type=F32)
          for c in chunks for hq in range(nqk)}
    decay = {ch: jnp.where(incl, jnp.exp(jnp.where(incl, g_col[ch] - g_row[ch], 0.0)), 0.0) for ch in chains}
    p = {(c, hv): -jnp.where(strict, kk[c, hv // 2][(hv % 2) * C:(hv % 2 + 1) * C] * decay[c, hv], 0.0)
         for c, hv in chains}
    intra = {(c, hv): jnp.where(incl, kk[c, hv // 2][2 * C:] * decay[c, hv], 0.0) for c, hv in chains}
    t = {ch: eye + p[ch] for ch in chains}
    pk = {ch: dot(p[ch], p[ch]) for ch in chains}
    steps = int(math.log2(C)) - 1
    for lvl in range(steps):
        if lvl < steps - 1:
            pp = {ch: dot(pk[ch], jnp.concatenate([pk[ch], t[ch]], axis=1)) for ch in chains}
            pk = {ch: pp[ch][:, :C] for ch in chains}
            t = {ch: t[ch] + pp[ch][:, C:] for ch in chains}
        else:
            t = {ch: t[ch] + dot(pk[ch], t[ch]) for ch in chains}
    eg = {ch: jnp.exp(g_col[ch]) for ch in chains}
    uw = {(c, hv): dot(t[c, hv], jnp.concatenate([v_ref[sls[c], hsl[hv]] * b_col[c, hv],
                                                  kb[c, hv] * eg[c, hv]], axis=1)) for c, hv in chains}
    iu = {ch: dot(intra[ch], uw[ch]) for ch in chains}
    ku = {(c, hv): lax.dot_general(kc[c, hv // 2] * jnp.exp(g_last[c, hv] - g_col[c, hv]), uw[c, hv],
                                   (((0,), (0,)), ((), ())), preferred_element_type=F32) for c, hv in chains}
    for n, (c, hv) in enumerate(chains):
        qm_ref[n, 0:C, :] = (qc[c, hv // 2] * eg[c, hv] - iu[c, hv][:, Dh:]).astype(BF16)
        qm_ref[n, C:, :] = (-ku[c, hv][:, Dh:]).astype(BF16)
        on_ref[n, 0:C, :] = iu[c, hv][:, :Dh]
        on_ref[n, C:, :] = ku[c, hv][:, :Dh]

    states = [s_ref[hv] for hv in heads]
    for c in (reversed(chunks) if reverse else chunks):
        for hv in heads:
            n = c * len(heads) + hv
            res = dot(qm_ref[n], states[hv].astype(BF16)) + on_ref[n]
            o_ref[sls[c], hsl[hv]] = res[:C]
            states[hv] = states[hv] * jnp.exp(g_last[c, hv]) + res[C:]
    for hv in heads:
        s_ref[hv] = states[hv]


def _gdn_chunk(qk, v, gcol, grow, *, direction):
    Lp = qk.shape[0]
    Dh = GDN_HEAD_DIM
    C = GDN_CHUNK
    tb = _pick(Lp, (GDN_ROWS_PER_STEP,))
    nqk = GDN_QK_HEADS_PER_STEP
    nblk = Lp // tb
    nchains = 2 * nqk * tb // C
    reverse = direction == 1

    def blk(i):
        return nblk - 1 - i if reverse else i

    return pl.pallas_call(
        functools.partial(_gdn_chunk_body, reverse=reverse, nchunks=tb // C, nqk=nqk),
        grid=(GDN_QK_HEADS // nqk, nblk),
        in_specs=[pl.BlockSpec((tb, nqk * Dh), lambda h, i: (blk(i), h)),
                  pl.BlockSpec((tb, nqk * Dh), lambda h, i: (blk(i), GDN_QK_HEADS // nqk + h)),
                  pl.BlockSpec((tb, 2 * nqk * Dh), lambda h, i: (blk(i), h)),
                  pl.BlockSpec((None, nqk, tb, 4), lambda h, i: (direction, h, blk(i), 0)),
                  pl.BlockSpec((None, nqk, 2, tb), lambda h, i: (direction, h, 0, blk(i)))],
        out_specs=pl.BlockSpec((tb, 2 * nqk * Dh), lambda h, i: (blk(i), h)),
        out_shape=jax.ShapeDtypeStruct((Lp, GDN_V_DIM), F32),
        scratch_shapes=[pltpu.VMEM((2 * nqk, Dh, Dh), F32),
                        pltpu.VMEM((nchains, C + Dh, Dh), BF16),
                        pltpu.VMEM((nchains, C + Dh, Dh), F32)],
        compiler_params=_cparams("parallel", "arbitrary"),
        name="gdn_chunk_bwd" if reverse else "gdn_chunk_fwd",
    )(qk, qk, v, gcol, grow)


def _gdn_norm_body(of_ref, ob_ref, z_ref, nw_ref, o_ref):
    tc = of_ref.shape[1]
    for hh in range(tc // LANES):
        sl = slice(hh * LANES, (hh + 1) * LANES)
        o = of_ref[:, sl] + ob_ref[:, sl]
        o = o * lax.rsqrt(jnp.mean(o * o, axis=-1, keepdims=True) + RMS_EPS)
        z = z_ref[:, sl]
        o_ref[:, sl] = (o * nw_ref[...] * (z * _sigmoid(z))).astype(BF16)


def _gdn_norm(o_f, o_b, z, norm_w):
    Lp, W = o_f.shape
    tr = _pick(Lp, (768, 512, 256))
    tc = 512
    spec = pl.BlockSpec((tr, tc), lambda i, j: (i, j))
    return pl.pallas_call(
        _gdn_norm_body,
        grid=(Lp // tr, W // tc),
        in_specs=[spec, spec, spec, pl.BlockSpec((1, LANES), lambda i, j: (0, 0))],
        out_specs=spec,
        out_shape=jax.ShapeDtypeStruct((Lp, W), BF16),
        compiler_params=_cparams("parallel", "parallel"),
        name="gdn_norm",
    )(o_f, o_b, z, norm_w.reshape(1, LANES))


def _gdn_mixer(hb, h, L, w_in, conv_w, a_log_f, a_log_b, dt_bias_f, dt_bias_b, norm_w, w_out, ln_g, ln_b):
    Lp = hb.shape[0]
    qkv_pre = _proj(hb, w_in, 0, GDN_CONV_CH)
    z = _proj(hb, w_in, GDN_CONV_CH, GDN_V_DIM)
    graw = _proj(hb, w_in, GDN_CONV_CH + GDN_V_DIM, 4 * GDN_V_HEADS)
    zero_b = jnp.zeros((1, GDN_CONV_CH), F32)
    qk = _conv(qkv_pre, conv_w, zero_b, L=L, col0=0, ncols=2 * GDN_QK_DIM, silu=True, l2norm=True)
    v = _conv(qkv_pre, conv_w, zero_b, L=L, col0=2 * GDN_QK_DIM, ncols=GDN_V_DIM, silu=True, l2norm=False)
    gpk = _gdn_gates(graw, a_log_f, a_log_b, dt_bias_f, dt_bias_b, L=L)
    g5 = gpk.reshape(Lp, 2, 2, GDN_QK_HEADS, 2)
    gcol = jnp.transpose(g5, (1, 3, 0, 2, 4)).reshape(2, GDN_QK_HEADS, Lp, 4)
    grow = jnp.transpose(g5[:, :, 0], (1, 2, 3, 0))
    o_f = _gdn_chunk(qk, v, gcol, grow, direction=0)
    o_b = _gdn_chunk(qk, v, gcol, grow, direction=1)
    a = _gdn_norm(o_f, o_b, z, norm_w)
    return _out_ln(a, w_out, h, ln_g, ln_b)


def _ssd_dt_body(raw_ref, alog_ref, dtb_ref, dt_ref, acs_ref, *, L, tr, nheads):
    i = pl.program_id(0)
    lane = lax.broadcasted_iota(jnp.int32, (1, 2 * nheads), 1)
    dt = _softplus(raw_ref[...] + dtb_ref[...])
    dt = jnp.where(_row_ids(i, tr) < L, dt, 0.0)
    dt_ref[...] = dt
    da = dt * (-jnp.exp(alog_ref[...]))
    C = SSD_CHUNK
    lo, up = _tri(C, False), _tri(C, True)
    for c in range(tr // C):
        blk = da[c * C:(c + 1) * C]
        pre = jnp.dot(lo, blk, precision=HIGHEST, preferred_element_type=F32)
        suf = jnp.dot(up, blk, precision=HIGHEST, preferred_element_type=F32)
        acs_ref[c * C:(c + 1) * C, :] = jnp.where(lane < nheads, pre, suf)


def _ssd_dt(raw, a_log_f, a_log_b, dt_bias_f, dt_bias_b, *, L):
    Lp, W = raw.shape
    alog = jnp.concatenate([a_log_f, a_log_b]).reshape(1, W)
    dtb = jnp.concatenate([dt_bias_f, dt_bias_b]).reshape(1, W)
    tr = _pick(Lp, (256,))
    spec = pl.BlockSpec((tr, W), lambda i: (i, 0))
    par = pl.BlockSpec((1, W), lambda i: (0, 0))
    return pl.pallas_call(
        functools.partial(_ssd_dt_body, L=L, tr=tr, nheads=W // 2),
        grid=(Lp // tr,),
        in_specs=[spec, par, par],
        out_specs=[spec, spec],
        out_shape=[jax.ShapeDtypeStruct((Lp, W), F32)] * 2,
        compiler_params=_cparams("parallel"),
        name="ssd_dt",
    )(raw, alog, dtb)


def _expand_heads(col, width):
    rows, R = col.shape
    rr = lax.broadcasted_iota(jnp.int32, (R, R * width), 0)
    cc = lax.broadcasted_iota(jnp.int32, (R, R * width), 1)
    e = jnp.where(cc // width == rr, 1.0, 0.0).astype(BF16)
    hi = col.astype(BF16)
    r1 = col - hi.astype(F32)
    mid = r1.astype(BF16)
    lo = (r1 - mid.astype(F32)).astype(BF16)
    dot = functools.partial(jnp.dot, preferred_element_type=F32)
    return dot(hi, e) + dot(mid, e) + dot(lo, e)


def _ssd_chunk_body(x_ref, b_ref, c_ref, scol_ref, srow_ref, o_ref, s_ref, *, reverse, nchunks, R):
    i = pl.program_id(1)

    @pl.when(i == 0)
    def _():
        s_ref[...] = jnp.zeros_like(s_ref)

    C = SSD_CHUNK
    P = SSD_HEAD_DIM
    rr = lax.broadcasted_iota(jnp.int32, (C, C), 0)
    cc = lax.broadcasted_iota(jnp.int32, (C, C), 1)
    incl = (rr <= cc) if reverse else (rr >= cc)
    last = 0 if reverse else C - 1
    chunks = range(nchunks)
    sls = [slice(c * C, (c + 1) * C) for c in chunks]
    dot = functools.partial(jnp.dot, preferred_element_type=F32)
    bm = [b_ref[sl, :] for sl in sls]
    cm = [c_ref[sl, :] for sl in sls]
    a_col = [scol_ref[sl, R:2 * R] for sl in sls]
    a_row = [srow_ref[:, sl] for sl in sls]
    a_last = [a[last:last + 1, :] for a in a_col]
    cb = [lax.dot_general(cm[c], bm[c], (((1,), (1,)), ((), ())), preferred_element_type=F32) for c in chunks]
    xdt = [x_ref[sls[c], :] * _expand_heads(scol_ref[sls[c], 0:R], P) for c in chunks]
    lmat = {(c, r): jnp.where(incl, jnp.exp(jnp.where(incl, a_col[c][:, r:r + 1] - a_row[c][r:r + 1, :], 0.0)), 0.0)
            for c in chunks for r in range(R)}
    for c in chunks:
        for r in range(R):
            o_ref[sls[c], r * P:(r + 1) * P] = dot(cb[c] * lmat[c, r], xdt[c][:, r * P:(r + 1) * P])
    st = [lax.dot_general(bm[c], xdt[c] * _expand_heads(jnp.exp(a_last[c] - a_col[c]), P),
                          (((0,), (0,)), ((), ())), preferred_element_type=F32) for c in chunks]
    S = s_ref[...]
    for c in (reversed(chunks) if reverse else chunks):
        o_ref[sls[c], :] += dot(cm[c], S) * _expand_heads(jnp.exp(a_col[c]), P)
        S = S * _expand_heads(jnp.exp(a_last[c]), P) + st[c]
    s_ref[...] = S


def _ssd_chunk(xbc, scol, srow, *, direction, d_inner):
    Lp = xbc.shape[0]
    G, N, P = SSD_GROUPS, SSD_STATE, SSD_HEAD_DIM
    R = d_inner // P // G
    gw = R * P
    tb = _pick(Lp, (SSD_ROWS_PER_STEP, 256))
    nblk = Lp // tb
    reverse = direction == 1
    b0 = d_inner // N
    c0 = b0 + G

    def blk(i):
        return nblk - 1 - i if reverse else i

    return pl.pallas_call(
        functools.partial(_ssd_chunk_body, reverse=reverse, nchunks=tb // SSD_CHUNK, R=R),
        grid=(G, nblk),
        in_specs=[pl.BlockSpec((tb, gw), lambda g, i: (blk(i), g)),
                  pl.BlockSpec((tb, N), lambda g, i: (blk(i), b0 + g)),
                  pl.BlockSpec((tb, N), lambda g, i: (blk(i), c0 + g)),
                  pl.BlockSpec((None, None, tb, 2 * R), lambda g, i: (direction, g, blk(i), 0)),
                  pl.BlockSpec((None, None, R, tb), lambda g, i: (direction, g, 0, blk(i)))],
        out_specs=pl.BlockSpec((tb, gw), lambda g, i: (blk(i), g)),
        out_shape=jax.ShapeDtypeStruct((Lp, d_inner), F32),
        scratch_shapes=[pltpu.VMEM((N, gw), F32)],
        compiler_params=_cparams("parallel", "arbitrary"),
        name="ssd_chunk_bwd" if reverse else "ssd_chunk_fwd",
    )(xbc, xbc, xbc, scol, srow)


def _ssd_norm_body(yf_ref, yb_ref, x_ref, z_ref, dsk_ref, nw_ref, o_ref):
    z = z_ref[...]
    y = (yf_ref[...] + yb_ref[...] + x_ref[...] * dsk_ref[...]) * (z * _sigmoid(z))
    y = y * lax.rsqrt(jnp.mean(y * y, axis=-1, keepdims=True) + RMS_EPS)
    o_ref[...] = (y * nw_ref[...]).astype(BF16)


def _ssd_norm(y_f, y_b, xbc, z, d_skip_row, norm_w):
    Lp, W = y_f.shape
    gw = W // SSD_GROUPS
    tr = _pick(Lp, (768, 512, 256))
    spec = pl.BlockSpec((tr, gw), lambda i, j: (i, j))
    par = pl.BlockSpec((1, gw), lambda i, j: (0, j))
    return pl.pallas_call(
        _ssd_norm_body,
        grid=(Lp // tr, SSD_GROUPS),
        in_specs=[spec, spec, spec, spec, par, par],
        out_specs=spec,
        out_shape=jax.ShapeDtypeStruct((Lp, W), BF16),
        compiler_params=_cparams("parallel", "parallel"),
        name="ssd_norm",
    )(y_f, y_b, xbc, z, d_skip_row, norm_w.reshape(1, W))


def _ssd_mixer(hb, h, L, w_in, conv_w, conv_b, a_log_f, a_log_b, dt_bias_f, dt_bias_b, d_skip, norm_w,
               w_out, ln_g, ln_b):
    Lp = hb.shape[0]
    d_inner = w_out[0].shape[1]
    conv_ch = conv_w.shape[1]
    nheads = d_inner // SSD_HEAD_DIM
    G = SSD_GROUPS
    R = nheads // G
    z = _proj(hb, w_in, 0, d_inner)
    xbc_pre = _proj(hb, w_in, d_inner, conv_ch)
    dt_raw = _proj(hb, w_in, d_inner + conv_ch, 2 * nheads)
    xbc = _conv(xbc_pre, conv_w, conv_b.reshape(1, conv_ch), L=L, col0=0, ncols=conv_ch, silu=True, l2norm=False)
    dt, acs = _ssd_dt(dt_raw, a_log_f, a_log_b, dt_bias_f, dt_bias_b, L=L)
    dt4 = jnp.transpose(dt.reshape(Lp, 2, G, R), (1, 2, 0, 3))
    acs4 = jnp.transpose(acs.reshape(Lp, 2, G, R), (1, 2, 0, 3))
    scol = jnp.concatenate([dt4, acs4], axis=-1)
    srow = jnp.transpose(acs4, (0, 1, 3, 2))
    y_f = _ssd_chunk(xbc, scol, srow, direction=0, d_inner=d_inner)
    y_b = _ssd_chunk(xbc, scol, srow, direction=1, d_inner=d_inner)
    d_skip_row = jnp.repeat(d_skip, SSD_HEAD_DIM).reshape(1, d_inner)
    a = _ssd_norm(y_f, y_b, xbc, z, d_skip_row, norm_w)
    return _out_ln(a, w_out, h, ln_g, ln_b)


def _fft_plan(L):
    N2 = _pick(L, (72, 64, 48, 32, 24, 16, 8))
    T1 = L // N2
    N1 = 2 * T1
    n = N1 * N2
    K1 = N1 // 2 + 1
    k1 = np.arange(K1, dtype=np.float64)[:, None]

    def stage1(T):
        th = 2.0 * np.pi * k1 * np.arange(T, dtype=np.float64)[None, :] / N1
        return np.cos(th), -np.sin(th)

    t2 = np.arange(N2, dtype=np.float64)[None, :]
    k2 = np.arange(N2, dtype=np.float64)[:, None]
    g = np.exp(-2j * np.pi * (t2 * k2 / N2)[None] - 2j * np.pi * (t2[None] * k1[:, :, None]) / n)
    gi = np.conj(np.transpose(g, (0, 2, 1)))

    def stack(m):
        return np.concatenate([np.concatenate([m.real, -m.imag], axis=2),
                               np.concatenate([m.imag, m.real], axis=2)], axis=1)

    t1 = np.arange(T1, dtype=np.float64)[:, None]
    kk = np.arange(K1, dtype=np.float64)[None, :]
    cw = np.where((kk == 0) | (kk == N1 // 2), 1.0, 2.0)
    th = 2.0 * np.pi * t1 * kk / N1
    f32 = lambda a: np.asarray(a, np.float32)
    fd_r, fd_i = stage1(T1)
    ff_r, ff_i = stage1(N1)
    return dict(N2=N2, T1=T1, N1=N1, K1=K1,
                fd_r=f32(fd_r), fd_i=f32(fd_i), ff_r=f32(ff_r), ff_i=f32(ff_i),
                g=f32(stack(g)), gi=f32(stack(gi)),
                fin_r=f32(cw * np.cos(th) / n), fin_i=f32(-cw * np.sin(th) / n))


SUBLANES = 8


def _dft1_body(fr_ref, fi_ref, x_ref, yr_ref, yi_ref):
    fr, fi = fr_ref[...], fi_ref[...]
    for s in range(SUBLANES):
        x = x_ref[:, s, :]
        yr_ref[:, s, :] = jnp.dot(fr, x, preferred_element_type=F32)
        yi_ref[:, s, :] = jnp.dot(fi, x, preferred_element_type=F32)


def _dft1(fr, fi, x3, col0, ncols):
    K1, T = fr.shape
    N2 = x3.shape[1]
    tc = _pick(ncols, (1024, 512, 256, 128))
    assert col0 % tc == 0 and N2 % SUBLANES == 0
    c0 = col0 // tc
    mat = pl.BlockSpec((K1, T), lambda j, c: (0, 0))
    out = pl.BlockSpec((K1, SUBLANES, tc), lambda j, c: (0, j, c))
    return pl.pallas_call(
        _dft1_body,
        grid=(N2 // SUBLANES, ncols // tc),
        in_specs=[mat, mat, pl.BlockSpec((T, SUBLANES, tc), lambda j, c: (0, j, c0 + c))],
        out_specs=[out, out],
        out_shape=[jax.ShapeDtypeStruct((K1, N2, ncols), F32)] * 2,
        compiler_params=_cparams("parallel", "parallel"),
        name="dft_stage1",
    )(fr, fi, x3)


def _dft2_body(g_ref, yr_ref, yi_ref, z_ref):
    y = jnp.concatenate([yr_ref[...], yi_ref[...]], axis=0)
    z_ref[...] = jnp.dot(g_ref[...], y, preferred_element_type=F32)


def _dft2(g, yr, yi, N2):
    K1 = g.shape[0]
    C = yr.shape[1]
    tc = _pick(C, (2048, 1024, 512, 256, 128))
    yspec = pl.BlockSpec((N2, tc), lambda a, j: (a, j))
    return pl.pallas_call(
        _dft2_body,
        grid=(K1, C // tc),
        in_specs=[pl.BlockSpec((None, 2 * N2, 2 * N2), lambda a, j: (a, 0, 0)), yspec, yspec],
        out_specs=pl.BlockSpec((2 * N2, tc), lambda a, j: (a, j)),
        out_shape=jax.ShapeDtypeStruct((K1 * 2 * N2, C), F32),
        compiler_params=_cparams("parallel", "parallel"),
        name="dft_stage2",
    )(g, yr, yi)


def _spec_mul_body(g_ref, gi_ref, yr_ref, yi_ref, h_ref, qr_ref, qi_ref, *, N2):
    y = jnp.concatenate([yr_ref[...], yi_ref[...]], axis=0)
    z = jnp.dot(g_ref[...], y, preferred_element_type=F32)
    zr, zi = z[:N2], z[N2:]
    hr, hi = h_ref[0:N2, :], h_ref[N2:2 * N2, :]
    p = jnp.concatenate([zr * hr - zi * hi, zr * hi + zi * hr], axis=0)
    q = jnp.dot(gi_ref[...], p, preferred_element_type=F32)
    qr_ref[...] = q[:N2]
    qi_ref[...] = q[N2:]


def _spec_mul(g, gi, yr, yi, hspec, order, N2):
    K1 = g.shape[0]
    C = yr.shape[1]
    mat = pl.BlockSpec((None, 2 * N2, 2 * N2), lambda a: (a, 0, 0))
    yspec = pl.BlockSpec((N2, C), lambda a: (a, 0))
    return pl.pallas_call(
        functools.partial(_spec_mul_body, N2=N2),
        grid=(K1,),
        in_specs=[mat, mat, yspec, yspec, pl.BlockSpec((2 * N2, C), lambda a: (a, order))],
        out_specs=[yspec, yspec],
        out_shape=[jax.ShapeDtypeStruct((K1 * N2, C), F32)] * 2,
        compiler_params=_cparams("parallel"),
        name="spec_mul",
    )(g, gi, yr, yi, hspec)


def _idft_gate_body(fr_ref, fi_ref, qr_ref, qi_ref, z_ref, gate_ref, skip_ref, o_ref):
    fr, fi = fr_ref[...], fi_ref[...]
    for s in range(SUBLANES):
        y = jnp.dot(fr, qr_ref[:, s, :], preferred_element_type=F32)
        y = y + jnp.dot(fi, qi_ref[:, s, :], preferred_element_type=F32)
        o_ref[:, s, :] = (gate_ref[:, s, :] * (y + z_ref[:, s, :] * skip_ref[...])).astype(o_ref.dtype)


def _idft_gate(fr, fi, qr3, qi3, z3, zcol, gate3, gcol, skip, out_dtype):
    T1, K1 = fr.shape
    N2, C = qr3.shape[1], qr3.shape[2]
    tc = _pick(C, (1024, 512, 256, 128))
    assert zcol % tc == 0 and gcol % tc == 0
    mat = pl.BlockSpec((T1, K1), lambda j, c: (0, 0))
    qs = pl.BlockSpec((K1, SUBLANES, tc), lambda j, c: (0, j, c))
    return pl.pallas_call(
        _idft_gate_body,
        grid=(N2 // SUBLANES, C // tc),
        in_specs=[mat, mat, qs, qs,
                  pl.BlockSpec((T1, SUBLANES, tc), lambda j, c: (0, j, zcol // tc + c)),
                  pl.BlockSpec((T1, SUBLANES, tc), lambda j, c: (0, j, gcol // tc + c)),
                  pl.BlockSpec((1, tc), lambda j, c: (0, c))],
        out_specs=pl.BlockSpec((T1, SUBLANES, tc), lambda j, c: (0, j, c)),
        out_shape=jax.ShapeDtypeStruct((T1, N2, C), out_dtype),
        compiler_params=_cparams("parallel", "parallel"),
        name="idft_gate",
    )(fr, fi, qr3, qi3, z3, gate3, skip.reshape(1, C))


def _hy_filter_body(feat_ref, w1_ref, b1_ref, f1_ref, w2_ref, b2_ref, f2_ref, w3_ref, b3_ref, dl_ref, o_ref,
                    *, L, tr):
    i = pl.program_id(0)
    feats = feat_ref[...]
    z = jnp.sin(f1_ref[...] * (jnp.dot(feats, w1_ref[...], preferred_element_type=F32) + b1_ref[...]))
    z = jnp.sin(f2_ref[...] * (jnp.dot(z, w2_ref[...], preferred_element_type=F32) + b2_ref[...]))
    r = jnp.dot(z, w3_ref[...], preferred_element_type=F32) + b3_ref[...]
    r = r * jnp.exp(-feats[:, 0:1] * dl_ref[...])
    o_ref[...] = jnp.where(_row_ids(i, tr) == L, 0.0, r)


def _hy_filter(L, fw1, fb1, freq1, fw2, fb2, freq2, fw3, fb3, width):
    pos = np.arange(2 * L)
    pos = np.where(pos < L, pos, (2 * L - pos) % L).astype(np.float64)
    tt = pos / (L - 1)
    w = 2.0 * np.pi * pos / L
    f = np.linspace(1e-4, HY_BANDS - 1, HY_BANDS)
    feats = np.concatenate([tt[:, None], np.cos(f[None] * w[:, None]), -np.sin(f[None] * w[:, None])], axis=1)
    feats = jnp.asarray(feats, F32)
    emb = feats.shape[1]
    hid = fw1.shape[1]
    n_ord = fw3.shape[1] // (2 * width)
    oc = n_ord * width
    deltas = np.abs(np.linspace(math.log(HY_TARGET) / HY_DECAY_PCT_SHORT,
                                math.log(HY_TARGET) / HY_DECAY_PCT_LONG, width))
    dl = jnp.asarray(np.tile(deltas, n_ord)[None], F32)
    tr = max(d for d in range(8, min(L, 512) + 1, 8) if L % d == 0)
    nt = L // tr
    row = lambda a: a.reshape(1, -1)
    full = lambda s: pl.BlockSpec(s, lambda i: (0, 0))
    return pl.pallas_call(
        functools.partial(_hy_filter_body, L=L, tr=tr),
        grid=(2 * nt,),
        in_specs=[pl.BlockSpec((tr, emb), lambda i: (i, 0)),
                  full((emb, hid)), full((1, hid)), full((1, hid)),
                  full((hid, hid)), full((1, hid)), full((1, hid)),
                  pl.BlockSpec((hid, oc), lambda i: (0, i // nt)),
                  pl.BlockSpec((1, oc), lambda i: (0, i // nt)),
                  full((1, oc))],
        out_specs=pl.BlockSpec((tr, oc), lambda i: (i, 0)),
        out_shape=jax.ShapeDtypeStruct((2 * L, oc), F32),
        compiler_params=_cparams("parallel"),
        name="hy_filter",
    )(feats, fw1, row(fb1), row(freq1), fw2, row(fb2), row(freq2), fw3, row(fb3), dl)


def _hyena_mixer(hb, h, L, w_in, conv_w, conv_b, fw1, fb1, freq1, fw2, fb2, freq2, fw3, fb3, filt_skip, w_out,
                 ln_g, ln_b):
    Lp = hb.shape[0]
    W = w_out[0].shape[1]
    n_ord = filt_skip.shape[0]
    u_pre = _proj(hb, w_in, 0, 3 * W)
    u = _conv(u_pre, conv_w, conv_b.reshape(1, 3 * W), L=L, col0=0, ncols=3 * W, silu=False, l2norm=False,
              out_rows=L)
    plan = _fft_plan(L)
    N2, T1, N1, K1 = plan["N2"], plan["T1"], plan["N1"], plan["K1"]
    cst = {k: jnp.asarray(v) for k, v in plan.items() if isinstance(v, np.ndarray)}
    resp = _hy_filter(L, fw1, fb1, freq1, fw2, fb2, freq2, fw3, fb3, W)
    hr, hi = _dft1(cst["ff_r"], cst["ff_i"], resp.reshape(N1, N2, n_ord * W), 0, n_ord * W)
    hspec = _dft2(cst["g"], hr.reshape(K1 * N2, n_ord * W), hi.reshape(K1 * N2, n_ord * W), N2)
    u3 = u.reshape(T1, N2, 3 * W)
    z3, zcol = u3, n_ord * W
    for o in range(n_ord):
        yr, yi = _dft1(cst["fd_r"], cst["fd_i"], z3, zcol, W)
        qr, qi = _spec_mul(cst["g"], cst["gi"], yr.reshape(K1 * N2, W), yi.reshape(K1 * N2, W), hspec, o, N2)
        z3 = _idft_gate(cst["fin_r"], cst["fin_i"], qr.reshape(K1, N2, W), qi.reshape(K1, N2, W), z3, zcol,
                        u3, o * W, filt_skip[o], F32)
        zcol = 0
    return _out_ln(z3.reshape(L, W), w_out, h, ln_g, ln_b, rows=L)


def _router_body(h_ref, w_ref, b_ref, o_ref, cnt_ref, base_ref, *, L):
    i = pl.program_id(0)

    @pl.when(i == 0)
    def _():
        base_ref[...] = jnp.zeros_like(base_ref)

    logits = jnp.dot(h_ref[...], w_ref[...], precision=HIGHEST, preferred_element_type=F32) + b_ref[...]
    tm = logits.shape[0]
    lane = lax.broadcasted_iota(jnp.int32, (tm, LANES), 1)
    neg = -jnp.inf
    G, E, PG = MOE_GROUPS, MOE_EXPERTS, MOE_PER_GROUP
    glog = jnp.where(lane < G, logits, neg)
    gmax = jnp.max(glog, axis=-1, keepdims=True)
    gsel = jnp.min(jnp.where(glog == gmax, lane, LANES), axis=-1, keepdims=True)
    gw = 1.0 / jnp.sum(jnp.exp(glog - gmax), axis=-1, keepdims=True)
    lo = G + gsel * PG
    elog = jnp.where(jnp.logical_and(lane >= lo, lane < lo + PG), logits, neg)
    m1 = jnp.max(elog, axis=-1, keepdims=True)
    i1 = jnp.min(jnp.where(elog == m1, lane, LANES), axis=-1, keepdims=True)
    elog2 = jnp.where(lane == i1, neg, elog)
    m2 = jnp.max(elog2, axis=-1, keepdims=True)
    i2 = jnp.min(jnp.where(elog2 == m2, lane, LANES), axis=-1, keepdims=True)
    e2 = jnp.exp(m2 - m1)
    w1 = gw / (1.0 + e2)
    w2 = gw * e2 / (1.0 + e2)
    valid = _row_ids(i, tm) < L
    oh1 = jnp.where(jnp.logical_and(valid, lane == i1 - G), 1.0, 0.0)
    oh2 = jnp.where(jnp.logical_and(valid, lane == i2 - G), 1.0, 0.0)
    both = oh1 + oh2
    rr = lax.broadcasted_iota(jnp.int32, (tm, tm), 0)
    cc = lax.broadcasted_iota(jnp.int32, (tm, tm), 1)
    earlier = jnp.where(rr > cc, 1.0, 0.0).astype(BF16)
    before = base_ref[...] + jnp.dot(earlier, both.astype(BF16), preferred_element_type=F32)
    rank1 = jnp.sum(before * oh1, axis=-1, keepdims=True)
    rank2 = jnp.sum(before * oh2, axis=-1, keepdims=True)
    base_ref[...] += jnp.sum(both, axis=0, keepdims=True)
    cnt_ref[...] = base_ref[...]
    cols = [(i1 - G).astype(F32), (i2 - G).astype(F32), w1, w2, rank1, rank2]
    out = jnp.zeros((tm, LANES), F32)
    for n, col in enumerate(cols):
        out = jnp.where(lane == n, col, out)
    o_ref[...] = out


def _router(h, wr_group, br_group, wr_expert, br_expert, *, L):
    Lp, D = h.shape
    used = MOE_GROUPS + MOE_EXPERTS
    w = jnp.pad(jnp.concatenate([wr_group, wr_expert], axis=1), ((0, 0), (0, LANES - used)))
    b = jnp.pad(jnp.concatenate([br_group, br_expert]), (0, LANES - used)).reshape(1, LANES)
    tm = _pick(Lp, (256,))
    return pl.pallas_call(
        functools.partial(_router_body, L=L),
        grid=(Lp // tm,),
        in_specs=[pl.BlockSpec((tm, D), lambda i: (i, 0)),
                  pl.BlockSpec((D, LANES), lambda i: (0, 0)),
                  pl.BlockSpec((1, LANES), lambda i: (0, 0))],
        out_specs=[pl.BlockSpec((tm, LANES), lambda i: (i, 0)),
                   pl.BlockSpec((1, LANES), lambda i: (0, 0))],
        out_shape=[jax.ShapeDtypeStruct((Lp, LANES), F32), jax.ShapeDtypeStruct((1, LANES), F32)],
        scratch_shapes=[pltpu.VMEM((1, LANES), F32)],
        compiler_params=_cparams("arbitrary"),
        name="router",
    )(h, w, b)


def _row_copy(src_hbm, dst_vmem, sem, src_row, dst_row):
    return pltpu.make_async_copy(src_hbm.at[pl.ds(src_row, 1)], dst_vmem.at[pl.ds(dst_row, 1)], sem)


def _gather_start(idx_ref, base, src_hbm, dst_vmem, sem, rows):
    def issue(s, carry):
        _row_copy(src_hbm, dst_vmem, sem, idx_ref[base + s], s).start()
        return carry

    lax.fori_loop(0, rows, issue, 0)


def _gather_wait(src_hbm, dst_vmem, sem, rows):
    pltpu.make_async_copy(src_hbm.at[pl.ds(0, rows)], dst_vmem.at[pl.ds(0, rows)], sem).wait()


def _expert_body(tok_ref, be_ref, first_ref, nu_ref, h3_hbm, wg_ref, wu_ref, wd_ref, o3_ref,
                 xbuf, xb, wgb, wub, wdb, sem, *, blk):
    b = pl.program_id(0)
    n_used = nu_ref[0]
    slot = b % 2

    @pl.when(jnp.logical_and(b == 0, n_used > 0))
    def _():
        _gather_start(tok_ref, 0, h3_hbm, xbuf.at[0], sem.at[0], blk)

    @pl.when(b + 1 < n_used)
    def _():
        _gather_start(tok_ref, (b + 1) * blk, h3_hbm, xbuf.at[1 - slot], sem.at[1 - slot], blk)

    @pl.when(b < n_used)
    def _():
        @pl.when(first_ref[b] == 1)
        def _():
            wgb[...] = wg_ref[...].astype(BF16)
            wub[...] = wu_ref[...].astype(BF16)
            wdb[...] = wd_ref[...].astype(BF16)

        _gather_wait(h3_hbm, xbuf.at[slot], sem.at[slot], blk)
        for s in range(xb.shape[1] // LANES):
            xb[:, s * LANES:(s + 1) * LANES] = _load_row_tiles(xbuf.at[slot], s).astype(BF16)
        x = xb[...]
        g = jnp.dot(x, wgb[...], preferred_element_type=F32)
        u = jnp.dot(x, wub[...], preferred_element_type=F32)
        hid = (g * _sigmoid(g)) * u
        _store_row_tiles(o3_ref, jnp.dot(hid.astype(BF16), wdb[...], preferred_element_type=F32))

    @pl.when(b >= n_used)
    def _():
        o3_ref[...] = jnp.zeros_like(o3_ref)


def _experts(h3, slot_tok, blk_expert, first, n_used, wg, wu, wd, layer):
    S = h3.shape[1]
    D = S * LANES
    FF = wg.shape[3]
    BLK = MOE_BLOCK
    nb = slot_tok.shape[0] // BLK
    wspec = lambda s: pl.BlockSpec((None, None) + s, lambda b, tok, be, fi, nu: (layer, be[b], 0, 0))
    return pl.pallas_call(
        functools.partial(_expert_body, blk=BLK),
        grid_spec=pltpu.PrefetchScalarGridSpec(
            num_scalar_prefetch=4,
            grid=(nb,),
            in_specs=[pl.BlockSpec(memory_space=pl.ANY), wspec((D, FF)), wspec((D, FF)), wspec((FF, D))],
            out_specs=pl.BlockSpec((BLK, S, LANES), lambda b, tok, be, fi, nu: (b, 0, 0)),
            scratch_shapes=[pltpu.VMEM((2, BLK, S, LANES), F32), pltpu.VMEM((BLK, D), BF16),
                            pltpu.VMEM((D, FF), BF16), pltpu.VMEM((D, FF), BF16), pltpu.VMEM((FF, D), BF16),
                            pltpu.SemaphoreType.DMA((2,))]),
        out_shape=jax.ShapeDtypeStruct((nb * BLK, S, LANES), F32),
        compiler_params=_cparams("arbitrary"),
        name="moe_experts",
    )(slot_tok, blk_expert, first, n_used, h3, wg, wu, wd)


def _combine_body(d0_ref, d1_ref, yb_hbm, rw_ref, h_ref, g_ref, b_ref, o_ref, ob_ref, buf, f_ref, sem, *, L, tm):
    i = pl.program_id(0)
    n = pl.num_programs(0)
    slot = i % 2

    def start(tile, sl):
        _gather_start(d0_ref, tile * tm, yb_hbm, buf.at[sl, 0], sem.at[sl], tm)
        _gather_start(d1_ref, tile * tm, yb_hbm, buf.at[sl, 1], sem.at[sl], tm)

    @pl.when(i == 0)
    def _():
        start(0, 0)

    @pl.when(i + 1 < n)
    def _():
        start(i + 1, 1 - slot)

    _gather_wait(yb_hbm, buf.at[slot, 0], sem.at[slot], tm)
    _gather_wait(yb_hbm, buf.at[slot, 1], sem.at[slot], tm)
    rw = rw_ref[...]
    valid = _row_ids(i, tm) < L
    w1 = jnp.where(valid, rw[:, 2:3], 0.0)
    w2 = jnp.where(valid, rw[:, 3:4], 0.0)
    for s in range(f_ref.shape[1] // LANES):
        f_ref[:, s * LANES:(s + 1) * LANES] = (_load_row_tiles(buf.at[slot, 0], s) * w1
                                               + _load_row_tiles(buf.at[slot, 1], s) * w2)
    y = _layer_norm_rows(DN_ALPHA * h_ref[...] + f_ref[...], g_ref[...], b_ref[...])
    o_ref[...] = y
    ob_ref[...] = y.astype(BF16)


def _combine_ln(yb3, dest0, dest1, rw, h, g, b, *, L):
    Lp, D = h.shape
    S = yb3.shape[1]
    tm = _pick(Lp, (256,))
    row = pl.BlockSpec((tm, D), lambda i, d0, d1: (i, 0))
    par = pl.BlockSpec((1, D), lambda i, d0, d1: (0, 0))
    return pl.pallas_call(
        functools.partial(_combine_body, L=L, tm=tm),
        grid_spec=pltpu.PrefetchScalarGridSpec(
            num_scalar_prefetch=2,
            grid=(Lp // tm,),
            in_specs=[pl.BlockSpec(memory_space=pl.ANY),
                      pl.BlockSpec((tm, LANES), lambda i, d0, d1: (i, 0)),
                      row, par, par],
            out_specs=[row, row],
            scratch_shapes=[pltpu.VMEM((2, 2, tm, S, LANES), F32), pltpu.VMEM((tm, D), F32),
                            pltpu.SemaphoreType.DMA((2,))]),
        out_shape=[jax.ShapeDtypeStruct((Lp, D), F32), jax.ShapeDtypeStruct((Lp, D), BF16)],
        compiler_params=_cparams("arbitrary"),
        name="moe_combine",
    )(dest0, dest1, yb3, rw, h, g.reshape(1, D), b.reshape(1, D))


def _moe(h, h3, L, layer, wr_group, br_group, wr_expert, br_expert, w_gate, w_up, w_down, ln_g, ln_b):
    Lp, D = h.shape
    E, BLK = MOE_EXPERTS, MOE_BLOCK
    rw, cnt = _router(h, wr_group, br_group, wr_expert, br_expert, L=L)
    counts = cnt[0, :E].astype(jnp.int32)
    padded = (counts + BLK - 1) // BLK * BLK
    ids = jnp.arange(E, dtype=jnp.int32)
    pend = jnp.sum(jnp.where(ids[None, :] <= ids[:, None], padded[None, :], 0), axis=1)
    eid = rw[:, 0:2].astype(jnp.int32)
    pstart_of = jnp.sum(jnp.where(ids < eid[..., None], padded, 0), axis=-1)
    dest = pstart_of + rw[:, 4:6].astype(jnp.int32)
    tok = lax.broadcasted_iota(jnp.int32, (Lp, 2), 0)
    dest = jnp.where(tok < L, dest, 0)
    nb = -(-2 * L // BLK) + E
    slot_tok = jnp.full((nb * BLK,), L, jnp.int32).at[dest[:L].reshape(-1)].set(tok[:L].reshape(-1))
    blk_start = jnp.arange(nb, dtype=jnp.int32) * BLK
    blk_expert = jnp.minimum(jnp.sum((pend[None, :] <= blk_start[:, None]).astype(jnp.int32), axis=1), E - 1)
    first = jnp.concatenate([jnp.ones((1,), jnp.int32), (blk_expert[1:] != blk_expert[:-1]).astype(jnp.int32)])
    n_used = (pend[-1:] // BLK).astype(jnp.int32)
    yb3 = _experts(h3, slot_tok, blk_expert, first, n_used, w_gate, w_up, w_down, layer)
    return _combine_ln(yb3, dest[:, 0], dest[:, 1], rw, h, ln_g, ln_b, L=L)


def kernel(x, meta_tokens, ln_mix_g, ln_mix_b, ln_ffn_g, ln_ffn_b, gdn_w_in, gdn_conv_w, gdn_a_log_f, gdn_a_log_b, gdn_dt_bias_f, gdn_dt_bias_b, gdn_norm_w, gdn_w_out, ssd_w_in, ssd_conv_w, ssd_conv_b, ssd_a_log_f, ssd_a_log_b, ssd_dt_bias_f, ssd_dt_bias_b, ssd_d_skip, ssd_norm_w, ssd_w_out, hy_w_in, hy_conv_w, hy_conv_b, hy_fw1, hy_fb1, hy_freq1, hy_fw2, hy_fb2, hy_freq2, hy_fw3, hy_fb3, hy_filt_skip, hy_w_out, moe_wr_group, moe_br_group, moe_wr_expert, moe_br_expert, moe_w_gate, moe_w_up, moe_w_down):
    assert x.shape[0] == 1
    n_meta = meta_tokens.shape[0]
    L = n_meta + x.shape[1]
    Lp = -(-L // 256) * 256
    depth = ln_mix_g.shape[0]
    h = jnp.concatenate([meta_tokens.astype(x.dtype), x[0]], axis=0)
    h = jnp.pad(h, ((0, Lp - L), (0, 0)))
    hb = h.astype(BF16)
    for i in range(depth):
        kind, j = i % N_MIXERS, i // N_MIXERS
        if kind == 0:
            h, h3 = _gdn_mixer(hb, h, L, (gdn_w_in, j), gdn_conv_w[j], gdn_a_log_f[j], gdn_a_log_b[j],
                               gdn_dt_bias_f[j], gdn_dt_bias_b[j], gdn_norm_w[j], (gdn_w_out, j),
                               ln_mix_g[i], ln_mix_b[i])
        elif kind == 1:
            h, h3 = _ssd_mixer(hb, h, L, (ssd_w_in, j), ssd_conv_w[j], ssd_conv_b[j], ssd_a_log_f[j],
                               ssd_a_log_b[j], ssd_dt_bias_f[j], ssd_dt_bias_b[j], ssd_d_skip[j],
                               ssd_norm_w[j], (ssd_w_out, j), ln_mix_g[i], ln_mix_b[i])
        else:
            h, h3 = _hyena_mixer(hb, h, L, (hy_w_in, j), hy_conv_w[j], hy_conv_b[j], hy_fw1[j], hy_fb1[j],
                                 hy_freq1[j], hy_fw2[j], hy_fb2[j], hy_freq2[j], hy_fw3[j], hy_fb3[j],
                                 hy_filt_skip[j], (hy_w_out, j), ln_mix_g[i], ln_mix_b[i])
        h, hb = _moe(h, h3, L, i, moe_wr_group[i], moe_br_group[i], moe_wr_expert[i], moe_br_expert[i],
                     moe_w_gate, moe_w_up, moe_w_down, ln_ffn_g[i], ln_ffn_b[i])
    return h[n_meta:L][None]
```
